```python
import jax, jax.numpy as jnp
from jax import lax
import numpy as np

D_MODEL = 1024
BATCH = 4
SEQ = 8192
DEPTH = 1

D_CONV = D_MODEL
CONV_A_WIDTH = 3
EXPAND = 2
D_INNER = EXPAND * D_MODEL
HEAD_DIM = 64
N_HEADS = D_INNER // HEAD_DIM
N_GROUPS = 4
D_STATE = 128
SSD_CONV_WIDTH = 4
CHUNK = 128
D_XBC = D_INNER + 2 * N_GROUPS * D_STATE
D_FF = 2816
FFN_CONV_WIDTH = 3
N_IN = 2 * D_MODEL + 3 * D_CONV + D_INNER + D_XBC + N_HEADS
EPS = 1e-5
DT_MIN = 1e-3
DT_MAX = 1e-1

kernel_name = "hybrid_shortconv_ssd_gated_merge_convffn"


def rmsnorm(x, w):
    xf = x.astype(jnp.float32)
    y = xf * lax.rsqrt(jnp.mean(xf * xf, axis=-1, keepdims=True) + EPS)
    return (y * w.astype(jnp.float32)).astype(x.dtype)


def causal_dwconv(x, w):
    k, c = w.shape
    return lax.conv_general_dilated(
        x, w[:, None, :].astype(x.dtype), window_strides=(1,), padding=[(k - 1, 0)],
        dimension_numbers=("NWC", "WIO", "NWC"), feature_group_count=c)


def ssd_chunked_scan(xh, dt, a, bmat, cmat):
    b, s, h, p = xh.shape
    g, n = bmat.shape[-2:]
    j = h // g
    c = s // CHUNK
    x_ = (xh.astype(jnp.float32) * dt[..., None]).reshape(b, c, CHUNK, g, j, p)
    log_a = jnp.moveaxis((dt * a).reshape(b, c, CHUNK, g, j), 2, -1)
    bc = bmat.astype(jnp.float32).reshape(b, c, CHUNK, g, n)
    cc = cmat.astype(jnp.float32).reshape(b, c, CHUNK, g, n)
    a_cum = jnp.cumsum(log_a, axis=-1)

    causal = jnp.tril(jnp.ones((CHUNK, CHUNK), dtype=bool))
    seg = a_cum[..., :, None] - a_cum[..., None, :]
    decay_in = jnp.exp(jnp.where(causal, seg, -jnp.inf))
    cb = jnp.einsum("bclgn,bcsgn->bcgls", cc, bc)
    y_diag = jnp.einsum("bcgjls,bcsgjp->bclgjp", cb[:, :, :, None] * decay_in, x_)

    decay_to_end = jnp.exp(a_cum[..., -1:] - a_cum)
    states = jnp.einsum("bclgn,bcgjl,bclgjp->bcgjpn", bc, decay_to_end, x_)
    chunk_decay = jnp.exp(a_cum[..., -1])

    def step(carry, inp):
        st, dec = inp
        return carry * dec[..., None, None] + st, carry

    init = jnp.zeros((b, g, j, p, n), jnp.float32)
    _, prev = lax.scan(step, init, (jnp.moveaxis(states, 1, 0), jnp.moveaxis(chunk_decay, 1, 0)))
    prev = jnp.moveaxis(prev, 0, 1)
    y_off = jnp.einsum("bclgn,bcgjpn,bcgjl->bclgjp", cc, prev, jnp.exp(a_cum))
    return (y_diag + y_off).reshape(b, s, h, p)


def gated_group_rmsnorm(y, z, w):
    bsz, s, d = y.shape
    yf = (y.astype(jnp.float32) * jax.nn.silu(z.astype(jnp.float32))).reshape(bsz, s, N_GROUPS, d // N_GROUPS)
    yf = yf * lax.rsqrt(jnp.mean(yf * yf, axis=-1, keepdims=True) + EPS)
    return (yf.reshape(bsz, s, d) * w.astype(jnp.float32)).astype(y.dtype)


def hybrid_mixer(u, w_in, conv_a_w, w_a_out, ssd_conv_w, ssd_conv_b, dt_bias, a_log,
                 d_skip, ssd_norm_w, w_s_out, w_o):
    bsz, s, _ = u.shape
    proj = u @ w_in
    sizes = [D_MODEL, D_MODEL, D_CONV, D_CONV, D_CONV, D_INNER, D_XBC, N_HEADS]
    offsets = np.cumsum(sizes)[:-1].tolist()
    gate_a, gate_s, b_a, c_a, v_a, z, xbc, dt_raw = jnp.split(proj, offsets, axis=-1)

    y_a = (b_a * causal_dwconv(c_a * v_a, conv_a_w)) @ w_a_out

    xbc = jax.nn.silu(causal_dwconv(xbc, ssd_conv_w) + ssd_conv_b)
    xs, bs, cs = jnp.split(xbc, [D_INNER, D_INNER + N_GROUPS * D_STATE], axis=-1)
    xh = xs.reshape(bsz, s, N_HEADS, HEAD_DIM)
    dt = jax.nn.softplus(dt_raw.astype(jnp.float32) + dt_bias.astype(jnp.float32))
    a = -jnp.exp(a_log.astype(jnp.float32))
    y = ssd_chunked_scan(xh, dt, a,
                         bs.reshape(bsz, s, N_GROUPS, D_STATE),
                         cs.reshape(bsz, s, N_GROUPS, D_STATE))
    y = y + d_skip.astype(jnp.float32)[:, None] * xh.astype(jnp.float32)
    y = y.reshape(bsz, s, D_INNER).astype(u.dtype)
    y_s = gated_group_rmsnorm(y, z, ssd_norm_w) @ w_s_out

    merged = jax.nn.sigmoid(gate_a) * y_a + jax.nn.sigmoid(gate_s) * y_s
    return merged @ w_o


def conv_gated_mlp(v, w_up, ffn_conv_w, ffn_conv_b, w_down):
    hv = v @ w_up
    h1, h3 = jnp.split(hv, 2, axis=-1)
    h1 = causal_dwconv(h1, ffn_conv_w) + ffn_conv_b
    return (jax.nn.silu(h1) * h3) @ w_down


def setup_inputs(seed: int = 0) -> dict:
    key = jax.random.key(seed)
    ks = jax.random.split(key, 24)
    f32 = jnp.float32

    def nrm(k, shape, scale):
        return jax.random.normal(k, shape, f32) * scale

    dt0 = jnp.exp(jax.random.uniform(ks[9], (DEPTH, N_HEADS), f32)
                  * (np.log(DT_MAX) - np.log(DT_MIN)) + np.log(DT_MIN))
    dt_bias = dt0 + jnp.log(-jnp.expm1(-dt0))
    return {
        "x": nrm(ks[0], (BATCH, SEQ, D_MODEL), 1.0),
        "norm_mix_w": 1.0 + nrm(ks[1], (DEPTH, D_MODEL), 0.02),
        "w_in": nrm(ks[2], (DEPTH, D_MODEL, N_IN), D_MODEL ** -0.5),
        "conv_a_w": nrm(ks[3], (DEPTH, CONV_A_WIDTH, D_CONV), CONV_A_WIDTH ** -0.5),
        "w_a_out": nrm(ks[4], (DEPTH, D_CONV, D_MODEL), D_CONV ** -0.5),
        "ssd_conv_w": nrm(ks[5], (DEPTH, SSD_CONV_WIDTH, D_XBC), SSD_CONV_WIDTH ** -0.5),
        "ssd_conv_b": nrm(ks[6], (DEPTH, D_XBC), 0.02),
        "dt_bias": dt_bias,
        "a_log": jnp.log(jax.random.uniform(ks[7], (DEPTH, N_HEADS), f32, 1.0, 16.0)),
        "d_skip": 1.0 + nrm(ks[8], (DEPTH, N_HEADS), 0.02),
        "ssd_norm_w": 1.0 + nrm(ks[10], (DEPTH, D_INNER), 0.02),
        "w_s_out": nrm(ks[11], (DEPTH, D_INNER, D_MODEL), D_INNER ** -0.5),
        "w_o": nrm(ks[12], (DEPTH, D_MODEL, D_MODEL), D_MODEL ** -0.5),
        "norm_ffn_w": 1.0 + nrm(ks[13], (DEPTH, D_MODEL), 0.02),
        "w_up": nrm(ks[14], (DEPTH, D_MODEL, 2 * D_FF), D_MODEL ** -0.5),
        "ffn_conv_w": nrm(ks[15], (DEPTH, FFN_CONV_WIDTH, D_FF), FFN_CONV_WIDTH ** -0.5),
        "ffn_conv_b": nrm(ks[16], (DEPTH, D_FF), 0.02),
        "w_down": nrm(ks[17], (DEPTH, D_FF, D_MODEL), D_FF ** -0.5),
        "final_norm_w": 1.0 + nrm(ks[18], (D_MODEL,), 0.02),
    }


def reference(x, norm_mix_w, w_in, conv_a_w, w_a_out, ssd_conv_w, ssd_conv_b, dt_bias,
              a_log, d_skip, ssd_norm_w, w_s_out, w_o, norm_ffn_w, w_up, ffn_conv_w,
              ffn_conv_b, w_down, final_norm_w):
    h = x
    for l in range(DEPTH):
        u = rmsnorm(h, norm_mix_w[l])
        h = h + hybrid_mixer(u, w_in[l], conv_a_w[l], w_a_out[l], ssd_conv_w[l], ssd_conv_b[l],
                             dt_bias[l], a_log[l], d_skip[l], ssd_norm_w[l], w_s_out[l], w_o[l])
        v = rmsnorm(h, norm_ffn_w[l])
        h = h + conv_gated_mlp(v, w_up[l], ffn_conv_w[l], ffn_conv_b[l], w_down[l])
    return rmsnorm(h, final_norm_w)
```

```python
import functools

import jax
import jax.numpy as jnp
from jax import lax
from jax.experimental import pallas as pl
from jax.experimental.pallas import tpu as pltpu

D_MODEL = 1024
D_CONV = D_MODEL
CONV_A_WIDTH = 3
D_INNER = 2048
HEAD_DIM = 64
N_HEADS = 32
N_GROUPS = 4
D_STATE = 128
SSD_CONV_WIDTH = 4
CHUNK = 128
D_XBC = D_INNER + 2 * N_GROUPS * D_STATE
D_FF = 2816
FFN_CONV_WIDTH = 3
EPS = 1e-5

LANES = 128
SUBLANES = 8
VMEM_PHYSICAL_BYTES = 64 * 1024 * 1024

HEADS_PAD = LANES
OFF_GATE_A = 0
OFF_GATE_S = OFF_GATE_A + D_MODEL
OFF_B_A = OFF_GATE_S + D_MODEL
OFF_C_A = OFF_B_A + D_CONV
OFF_V_A = OFF_C_A + D_CONV
OFF_Z = OFF_V_A + D_CONV
OFF_XBC = OFF_Z + D_INNER
OFF_DT = OFF_XBC + D_XBC
N_IN = OFF_DT + N_HEADS
N_IN_PAD = OFF_DT + HEADS_PAD

HEADS_PER_GROUP = N_HEADS // N_GROUPS
GROUP_WIDTH = HEADS_PER_GROUP * HEAD_DIM
HALO = SUBLANES

MIXER_TILE = 256
FFN_TILE = 256
MIXER_VMEM_LIMIT = 60 * 1024 * 1024
FFN_VMEM_LIMIT = 48 * 1024 * 1024

_F32 = jnp.float32
_BF16 = jnp.bfloat16


def _dot(a, b):
    return jnp.dot(a, b, preferred_element_type=_F32)


def _rmsnorm(x, w):
    return x * lax.rsqrt(jnp.mean(x * x, axis=-1, keepdims=True) + EPS) * w


def _silu(x):
    return x * jax.nn.sigmoid(x)


def _causal_conv(buf_ref, w_ref, width, rows):
    acc = None
    for k in range(width):
        start = HALO - (width - 1) + k
        term = w_ref[k:k + 1, :] * buf_ref[start:start + rows, :]
        acc = term if acc is None else acc + term
    return acc


def _split3(x):
    hi = x.astype(_BF16)
    r1 = x - hi.astype(_F32)
    mid = r1.astype(_BF16)
    lo = (r1 - mid.astype(_F32)).astype(_BF16)
    return hi, mid, lo


def _mixer_kernel(x_ref, nw_ref, win_ref, caw_ref, waout_ref, scw_ref, scb_ref, dtb_ref, alog_ref,
                  dskip_ref, snw_ref, wsout_ref, wo_ref, expand_ref,
                  o_ref,
                  u_ref, cv_ref, xbc_ref, xs_ref, b_ref, c_ref, z_ref, dt_ref, acum_ref, acumt_ref,
                  yn_ref, merged_ref, state_ref):
    tm = x_ref.shape[0]

    @pl.when(pl.program_id(1) == 0)
    def _():
        cv_ref[0:HALO, :] = jnp.zeros((HALO, D_CONV), _F32)
        xbc_ref[0:HALO, :] = jnp.zeros((HALO, D_XBC), _F32)
        state_ref[...] = jnp.zeros(state_ref.shape, _F32)

    x = x_ref[...]
    u_ref[...] = _rmsnorm(x, nw_ref[...]).astype(_BF16)

    def proj(off, width):
        return _dot(u_ref[...], win_ref[:, off:off + width])

    cv_ref[HALO:HALO + tm, :] = proj(OFF_C_A, D_CONV) * proj(OFF_V_A, D_CONV)
    conv_a = _causal_conv(cv_ref, caw_ref, CONV_A_WIDTH, tm)
    cv_ref[0:HALO, :] = cv_ref[tm:tm + HALO, :]
    y_a = _dot((proj(OFF_B_A, D_CONV) * conv_a).astype(_BF16), waout_ref[...])
    merged_ref[...] = jax.nn.sigmoid(proj(OFF_GATE_A, D_MODEL)) * y_a

    xbc_ref[HALO:HALO + tm, :] = proj(OFF_XBC, D_XBC)
    xbc = _silu(_causal_conv(xbc_ref, scw_ref, SSD_CONV_WIDTH, tm) + scb_ref[...])
    xbc_ref[0:HALO, :] = xbc_ref[tm:tm + HALO, :]
    xs_ref[...] = xbc[:, 0:D_INNER]
    b_ref[...] = xbc[:, D_INNER:D_INNER + N_GROUPS * D_STATE].astype(_BF16)
    c_ref[...] = xbc[:, D_INNER + N_GROUPS * D_STATE:D_XBC].astype(_BF16)
    z_ref[...] = proj(OFF_Z, D_INNER)
    dt_raw = proj(OFF_DT, HEADS_PAD) + dtb_ref[...]
    dt_ref[...] = jnp.maximum(dt_raw, 0.0) + jnp.log1p(jnp.exp(-jnp.abs(dt_raw)))

    a_row = -jnp.exp(alog_ref[...])
    row_id = lax.broadcasted_iota(jnp.int32, (CHUNK, CHUNK), 0)
    col_id = lax.broadcasted_iota(jnp.int32, (CHUNK, CHUNK), 1)
    causal = row_id >= col_id
    tril = causal.astype(_BF16)
    low_half = lax.broadcasted_iota(jnp.int32, (CHUNK, LANES), 1) < HEAD_DIM

    def chunk_body(ci, carry):
        r0 = pl.multiple_of(ci * CHUNK, CHUNK)
        rows = pl.ds(r0, CHUNK)
        dt = dt_ref[rows, :]
        hi, mid, lo = _split3(dt * a_row)
        a_cum = _dot(tril, hi) + _dot(tril, mid) + _dot(tril, lo)
        acum_ref[...] = a_cum
        acumt_ref[...] = a_cum.T
        a_last = a_cum[CHUNK - 1:CHUNK, :]
        decay_to_end = jnp.exp(a_last - a_cum)
        decay_in = jnp.exp(a_cum)
        per_head = jnp.concatenate([dt, dt * decay_to_end, decay_in], axis=0).astype(_BF16)
        per_chan = _dot(per_head, expand_ref[...])
        dt_full = per_chan[0:CHUNK]
        dtdecay_full = per_chan[CHUNK:2 * CHUNK]
        decay_in_full = per_chan[2 * CHUNK:3 * CHUNK]
        xs = xs_ref[rows, :]
        x_dt = (xs * dt_full).astype(_BF16)
        x_end = (xs * dtdecay_full).astype(_BF16)
        chunk_decay = decay_in_full[CHUNK - 1:CHUNK, :]

        for g in range(N_GROUPS):
            gcols = slice(g * GROUP_WIDTH, (g + 1) * GROUP_WIDTH)
            b_g = b_ref[rows, g * D_STATE:(g + 1) * D_STATE]
            c_g = c_ref[rows, g * D_STATE:(g + 1) * D_STATE]
            cb = lax.dot_general(c_g, b_g, (((1,), (1,)), ((), ())), preferred_element_type=_F32)
            state_prev = state_ref[:, gcols]
            y_off = _dot(c_g, state_prev.astype(_BF16))
            state_new = lax.dot_general(b_g, x_end[:, gcols], (((0,), (0,)), ((), ())),
                                        preferred_element_type=_F32)
            state_ref[:, gcols] = state_prev * chunk_decay[:, gcols] + state_new

            y_cols = []
            for jp in range(HEADS_PER_GROUP // 2):
                h0 = g * HEADS_PER_GROUP + 2 * jp
                pcols = slice(h0 * HEAD_DIM, (h0 + 2) * HEAD_DIM)
                decays = []
                for h in (h0, h0 + 1):
                    seg = acum_ref[:, h:h + 1] - acumt_ref[h:h + 1, :]
                    decays.append((cb * jnp.exp(jnp.where(causal, seg, -jnp.inf))).astype(_BF16))
                lhs = jnp.concatenate(decays, axis=1)
                x_pair = x_dt[:, pcols]
                zero = jnp.zeros_like(x_pair)
                rhs = jnp.concatenate([jnp.where(low_half, x_pair, zero),
                                       jnp.where(low_half, zero, x_pair)], axis=0)
                y_diag = _dot(lhs, rhs)
                y_cols.append(y_diag
                              + y_off[:, jp * LANES:(jp + 1) * LANES] * decay_in_full[:, pcols]
                              + dskip_ref[:, pcols] * xs[:, pcols])
            y_g = jnp.concatenate(y_cols, axis=1)
            yz = y_g * _silu(z_ref[rows, gcols])
            yn = yz * lax.rsqrt(jnp.mean(yz * yz, axis=-1, keepdims=True) + EPS) * snw_ref[:, gcols]
            yn_ref[rows, gcols] = yn.astype(_BF16)
        return carry

    lax.fori_loop(0, tm // CHUNK, chunk_body, 0)

    y_s = _dot(yn_ref[...], wsout_ref[...])
    merged = merged_ref[...] + jax.nn.sigmoid(proj(OFF_GATE_S, D_MODEL)) * y_s
    o_ref[...] = x_ref[...] + _dot(merged.astype(_BF16), wo_ref[...])


def _ffn_kernel(h_ref, nw_ref, wup_ref, fcw_ref, fcb_ref, wdown_ref, fnw_ref, o_ref, h1_ref):
    tm = h_ref.shape[0]

    @pl.when(pl.program_id(1) == 0)
    def _():
        h1_ref[0:HALO, :] = jnp.zeros((HALO, D_FF), _F32)

    h = h_ref[...]
    v = _rmsnorm(h, nw_ref[...]).astype(_BF16)
    h1_ref[HALO:HALO + tm, :] = _dot(v, wup_ref[:, 0:D_FF])
    h3 = _dot(v, wup_ref[:, D_FF:2 * D_FF])
    h1 = _causal_conv(h1_ref, fcw_ref, FFN_CONV_WIDTH, tm) + fcb_ref[...]
    h1_ref[0:HALO, :] = h1_ref[tm:tm + HALO, :]
    out = h + _dot((_silu(h1) * h3).astype(_BF16), wdown_ref[...])
    o_ref[...] = _rmsnorm(out, fnw_ref[...])


def _resident(shape):
    return pl.BlockSpec(shape, lambda b, s: (0,) * len(shape), pipeline_mode=pl.Buffered(1))


def _token_tile(tile):
    return pl.BlockSpec((None, tile, D_MODEL), lambda b, s: (b, s, 0))


def _mixer_call(x, nw, win, caw, waout, scw, scb, dtb, alog, dskip, snw, wsout, wo, expand):
    batch, seq, _ = x.shape
    tm = MIXER_TILE
    params = (nw, win, caw, waout, scw, scb, dtb, alog, dskip, snw, wsout, wo, expand)
    return pl.pallas_call(
        _mixer_kernel,
        out_shape=jax.ShapeDtypeStruct(x.shape, _F32),
        grid=(batch, seq // tm),
        in_specs=[_token_tile(tm)] + [_resident(p.shape) for p in params],
        out_specs=_token_tile(tm),
        scratch_shapes=[
            pltpu.VMEM((tm, D_MODEL), _BF16),
            pltpu.VMEM((HALO + tm, D_CONV), _F32),
            pltpu.VMEM((HALO + tm, D_XBC), _F32),
            pltpu.VMEM((tm, D_INNER), _F32),
            pltpu.VMEM((tm, N_GROUPS * D_STATE), _BF16),
            pltpu.VMEM((tm, N_GROUPS * D_STATE), _BF16),
            pltpu.VMEM((tm, D_INNER), _F32),
            pltpu.VMEM((tm, HEADS_PAD), _F32),
            pltpu.VMEM((CHUNK, HEADS_PAD), _F32),
            pltpu.VMEM((HEADS_PAD, CHUNK), _F32),
            pltpu.VMEM((tm, D_INNER), _BF16),
            pltpu.VMEM((tm, D_MODEL), _F32),
            pltpu.VMEM((D_STATE, D_INNER), _F32),
        ],
        compiler_params=pltpu.CompilerParams(
            dimension_semantics=("arbitrary", "arbitrary"),
            vmem_limit_bytes=MIXER_VMEM_LIMIT),
        name="mixer",
    )(x, *params)


def _ffn_call(h, nw, wup, fcw, fcb, wdown, fnw):
    batch, seq, _ = h.shape
    tm = FFN_TILE
    params = (nw, wup, fcw, fcb, wdown, fnw)
    return pl.pallas_call(
        _ffn_kernel,
        out_shape=jax.ShapeDtypeStruct(h.shape, _F32),
        grid=(batch, seq // tm),
        in_specs=[_token_tile(tm)] + [_resident(p.shape) for p in params],
        out_specs=_token_tile(tm),
        scratch_shapes=[pltpu.VMEM((HALO + tm, D_FF), _F32)],
        compiler_params=pltpu.CompilerParams(
            dimension_semantics=("arbitrary", "arbitrary"),
            vmem_limit_bytes=FFN_VMEM_LIMIT),
        name="ffn",
    )(h, *params)


def _row(v, pad_to=None):
    v = v.astype(_F32).reshape(1, -1)
    if pad_to is not None:
        v = jnp.pad(v, ((0, 0), (0, pad_to - v.shape[1])))
    return v


def kernel(x, norm_mix_w, w_in, conv_a_w, w_a_out, ssd_conv_w, ssd_conv_b, dt_bias, a_log, d_skip,
           ssd_norm_w, w_s_out, w_o, norm_ffn_w, w_up, ffn_conv_w, ffn_conv_b, w_down, final_norm_w):
    depth = w_in.shape[0]
    assert x.shape[1] % MIXER_TILE == 0 and x.shape[1] % FFN_TILE == 0
    assert MIXER_TILE % CHUNK == 0
    expand = (lax.broadcasted_iota(jnp.int32, (HEADS_PAD, D_INNER), 0)
              == lax.broadcasted_iota(jnp.int32, (HEADS_PAD, D_INNER), 1) // HEAD_DIM).astype(_BF16)
    h = x
    for l in range(depth):
        win = jnp.pad(w_in[l], ((0, 0), (0, N_IN_PAD - N_IN))).astype(_BF16)
        h = _mixer_call(
            h, _row(norm_mix_w[l]), win, conv_a_w[l].astype(_F32), w_a_out[l].astype(_BF16),
            ssd_conv_w[l].astype(_F32), _row(ssd_conv_b[l]), _row(dt_bias[l], HEADS_PAD),
            _row(a_log[l], HEADS_PAD), _row(jnp.repeat(d_skip[l], HEAD_DIM)), _row(ssd_norm_w[l]),
            w_s_out[l].astype(_BF16), w_o[l].astype(_BF16), expand)
        assert depth == 1
        h = _ffn_call(h, _row(norm_ffn_w[l]), w_up[l].astype(_BF16), ffn_conv_w[l].astype(_F32),
                      _row(ffn_conv_b[l]), w_down[l].astype(_BF16), _row(final_norm_w))
    return h
```

```python
import jax
import jax.numpy as jnp
from jax import lax
from jax.experimental import pallas as pl
from jax.experimental.pallas import tpu as pltpu

D_MODEL = 1024
D_CONV = D_MODEL
CONV_A_WIDTH = 3
D_INNER = 2048
HEAD_DIM = 64
N_HEADS = 32
N_GROUPS = 4
D_STATE = 128
SSD_CONV_WIDTH = 4
CHUNK = 128
D_XBC = D_INNER + 2 * N_GROUPS * D_STATE
D_FF = 2816
FFN_CONV_WIDTH = 3
EPS = 1e-5

LANES = 128
SUBLANES = 8
CONV_STRIDE = 4

HEADS_PAD = LANES
OFF_GATE_A = 0
OFF_GATE_S = OFF_GATE_A + D_MODEL
OFF_B_A = OFF_GATE_S + D_MODEL
OFF_C_A = OFF_B_A + D_CONV
OFF_V_A = OFF_C_A + D_CONV
OFF_Z = OFF_V_A + D_CONV
OFF_XBC = OFF_Z + D_INNER
OFF_DT = OFF_XBC + D_XBC
N_IN = OFF_DT + N_HEADS
N_IN_PAD = OFF_DT + HEADS_PAD

HEADS_PER_GROUP = N_HEADS // N_GROUPS
GROUP_WIDTH = HEADS_PER_GROUP * HEAD_DIM
GROUP_SLABS = GROUP_WIDTH // LANES
HALO = SUBLANES

MIXER_TILE = 256
FFN_TILE = 256
MIXER_VMEM_LIMIT = 60 * 1024 * 1024
FFN_VMEM_LIMIT = 48 * 1024 * 1024

_F32 = jnp.float32
_BF16 = jnp.bfloat16


def _dot(a, b):
    return jnp.dot(a, b, preferred_element_type=_F32)


def _rmsnorm(x, w):
    return x * lax.rsqrt(jnp.mean(x * x, axis=-1, keepdims=True) + EPS) * w


def _silu(x):
    return x * jax.nn.sigmoid(x)


def _to_slabs(buf_ref, value, row0):
    rows = value.shape[0]
    for s in range(value.shape[1] // LANES):
        buf_ref[s, row0:row0 + rows, :] = value[:, s * LANES:(s + 1) * LANES]


def _from_slabs(buf_ref, rows, slabs):
    return jnp.concatenate([buf_ref[s, rows, :] for s in slabs], axis=1)


def _causal_conv_slab(buf_ref, slab, w_ref, width, rows):
    n = rows // CONV_STRIDE
    first = HALO - (width - 1)
    reads = [buf_ref[slab, pl.ds(j, n, stride=CONV_STRIDE), :]
             for j in range(first, HALO + CONV_STRIDE)]
    cols = slice(slab * LANES, (slab + 1) * LANES)
    taps = [w_ref[k:k + 1, cols] for k in range(width)]
    outs = []
    for r in range(CONV_STRIDE):
        acc = taps[0] * reads[r]
        for k in range(1, width):
            acc = acc + taps[k] * reads[r + k]
        outs.append(acc)
    return outs


def _store_strided(dst_ref, slab, outs):
    n = outs[0].shape[0]
    for r, val in enumerate(outs):
        dst_ref[slab, pl.ds(r, n, stride=CONV_STRIDE), :] = val


def _keep_halo(buf_ref, rows):
    for s in range(buf_ref.shape[0]):
        buf_ref[s, 0:HALO, :] = buf_ref[s, rows:rows + HALO, :]


def _split3(x):
    hi = x.astype(_BF16)
    r1 = x - hi.astype(_F32)
    mid = r1.astype(_BF16)
    lo = (r1 - mid.astype(_F32)).astype(_BF16)
    return hi, mid, lo


def _mixer_kernel(x_ref, nw_ref, win_ref, caw_ref, waout_ref, scw_ref, scb_ref, dtb_ref, alog_ref,
                  dskip_ref, snw_ref, wsout_ref, wo_ref, expand_ref,
                  o_ref,
                  u_ref, cv_ref, conva_ref, xbc_ref, xs_ref, bc_ref, z_ref, dt_ref, acum_ref,
                  acumt_ref, yn_ref, merged_ref, state_ref):
    tm = x_ref.shape[0]
    n_xs_slabs = D_INNER // LANES

    @pl.when(pl.program_id(1) == 0)
    def _():
        cv_ref[:, 0:HALO, :] = jnp.zeros((cv_ref.shape[0], HALO, LANES), _F32)
        xbc_ref[:, 0:HALO, :] = jnp.zeros((xbc_ref.shape[0], HALO, LANES), _F32)
        state_ref[...] = jnp.zeros(state_ref.shape, _F32)

    x = x_ref[...]
    u_ref[...] = _rmsnorm(x, nw_ref[...]).astype(_BF16)

    def proj(off, width):
        return _dot(u_ref[...], win_ref[:, off:off + width])

    _to_slabs(cv_ref, proj(OFF_C_A, D_CONV) * proj(OFF_V_A, D_CONV), HALO)
    for s in range(D_CONV // LANES):
        _store_strided(conva_ref, s, _causal_conv_slab(cv_ref, s, caw_ref, CONV_A_WIDTH, tm))
    _keep_halo(cv_ref, tm)
    conv_a = _from_slabs(conva_ref, slice(0, tm), range(D_CONV // LANES))
    y_a = _dot((proj(OFF_B_A, D_CONV) * conv_a).astype(_BF16), waout_ref[...])
    merged_ref[...] = jax.nn.sigmoid(proj(OFF_GATE_A, D_MODEL)) * y_a

    _to_slabs(xbc_ref, proj(OFF_XBC, D_XBC), HALO)
    for s in range(D_XBC // LANES):
        cols = slice(s * LANES, (s + 1) * LANES)
        outs = [_silu(o + scb_ref[:, cols])
                for o in _causal_conv_slab(xbc_ref, s, scw_ref, SSD_CONV_WIDTH, tm)]
        if s < n_xs_slabs:
            _store_strided(xs_ref, s, outs)
        else:
            _store_strided(bc_ref, s - n_xs_slabs, outs)
    _keep_halo(xbc_ref, tm)
    z_ref[...] = _silu(proj(OFF_Z, D_INNER))
    dt_raw = proj(OFF_DT, HEADS_PAD) + dtb_ref[...]
    dt_ref[...] = jnp.maximum(dt_raw, 0.0) + jnp.log1p(jnp.exp(-jnp.abs(dt_raw)))

    a_row = -jnp.exp(alog_ref[...])
    row_id = lax.broadcasted_iota(jnp.int32, (CHUNK, CHUNK), 0)
    col_id = lax.broadcasted_iota(jnp.int32, (CHUNK, CHUNK), 1)
    causal = row_id >= col_id
    tril = causal.astype(_BF16)
    low_half = lax.broadcasted_iota(jnp.int32, (CHUNK, LANES), 1) < HEAD_DIM

    def chunk_body(ci, carry):
        r0 = pl.multiple_of(ci * CHUNK, CHUNK)
        rows = pl.ds(r0, CHUNK)
        dt = dt_ref[rows, :]
        hi, mid, lo = _split3(dt * a_row)
        a_cum = _dot(tril, hi) + _dot(tril, mid) + _dot(tril, lo)
        acum_ref[...] = a_cum
        acumt_ref[...] = a_cum.T
        a_last = a_cum[CHUNK - 1:CHUNK, :]
        decay_to_end = jnp.exp(a_last - a_cum)
        decay_in = jnp.exp(a_cum)
        per_head = jnp.concatenate([dt, dt * decay_to_end, decay_in], axis=0).astype(_BF16)
        per_chan = _dot(per_head, expand_ref[...])
        dt_full = per_chan[0:CHUNK]
        dtdecay_full = per_chan[CHUNK:2 * CHUNK]
        decay_in_full = per_chan[2 * CHUNK:3 * CHUNK]
        chunk_decay = decay_in_full[CHUNK - 1:CHUNK, :]

        for g in range(N_GROUPS):
            gcols = slice(g * GROUP_WIDTH, (g + 1) * GROUP_WIDTH)
            xs_g = _from_slabs(xs_ref, rows, range(g * GROUP_SLABS, (g + 1) * GROUP_SLABS))
            x_dt = (xs_g * dt_full[:, gcols]).astype(_BF16)
            x_end = (xs_g * dtdecay_full[:, gcols]).astype(_BF16)
            b_g = bc_ref[g, rows, :].astype(_BF16)
            c_g = bc_ref[N_GROUPS + g, rows, :].astype(_BF16)
            cb = lax.dot_general(c_g, b_g, (((1,), (1,)), ((), ())), preferred_element_type=_F32)
            state_prev = state_ref[:, gcols]
            y_off = _dot(c_g, state_prev.astype(_BF16))
            state_new = lax.dot_general(b_g, x_end, (((0,), (0,)), ((), ())),
                                        preferred_element_type=_F32)
            state_ref[:, gcols] = state_prev * chunk_decay[:, gcols] + state_new

            y_cols = []
            for jp in range(GROUP_SLABS):
                h0 = g * HEADS_PER_GROUP + 2 * jp
                pcols = slice(h0 * HEAD_DIM, (h0 + 2) * HEAD_DIM)
                lcols = slice(jp * LANES, (jp + 1) * LANES)
                decays = []
                for h in (h0, h0 + 1):
                    seg = acum_ref[:, h:h + 1] - acumt_ref[h:h + 1, :]
                    decays.append((cb * jnp.exp(jnp.where(causal, seg, -jnp.inf))).astype(_BF16))
                lhs = jnp.concatenate(decays, axis=1)
                x_pair = x_dt[:, lcols]
                zero = jnp.zeros_like(x_pair)
                rhs = jnp.concatenate([jnp.where(low_half, x_pair, zero),
                                       jnp.where(low_half, zero, x_pair)], axis=0)
                y_diag = _dot(lhs, rhs)
                y_cols.append(y_diag + y_off[:, lcols] * decay_in_full[:, pcols]
                              + dskip_ref[:, pcols] * xs_g[:, lcols])
            yz = jnp.concatenate(y_cols, axis=1) * z_ref[rows, gcols]
            yn = yz * lax.rsqrt(jnp.mean(yz * yz, axis=-1, keepdims=True) + EPS) * snw_ref[:, gcols]
            yn_ref[rows, gcols] = yn.astype(_BF16)
        return carry

    lax.fori_loop(0, tm // CHUNK, chunk_body, 0)

    y_s = _dot(yn_ref[...], wsout_ref[...])
    merged = merged_ref[...] + jax.nn.sigmoid(proj(OFF_GATE_S, D_MODEL)) * y_s
    o_ref[...] = x_ref[...] + _dot(merged.astype(_BF16), wo_ref[...])


def _ffn_kernel(h_ref, nw_ref, wup_ref, fcw_ref, fcb_ref, wdown_ref, fnw_ref, o_ref, h1_ref, act_ref):
    tm = h_ref.shape[0]

    @pl.when(pl.program_id(1) == 0)
    def _():
        h1_ref[:, 0:HALO, :] = jnp.zeros((h1_ref.shape[0], HALO, LANES), _F32)

    h = h_ref[...]
    v = _rmsnorm(h, nw_ref[...]).astype(_BF16)
    _to_slabs(h1_ref, _dot(v, wup_ref[:, 0:D_FF]), HALO)
    h3 = _dot(v, wup_ref[:, D_FF:2 * D_FF])
    for s in range(D_FF // LANES):
        cols = slice(s * LANES, (s + 1) * LANES)
        outs = [_silu(o + fcb_ref[:, cols])
                for o in _causal_conv_slab(h1_ref, s, fcw_ref, FFN_CONV_WIDTH, tm)]
        _store_strided(act_ref, s, outs)
    _keep_halo(h1_ref, tm)
    act = _from_slabs(act_ref, slice(0, tm), range(D_FF // LANES))
    out = h + _dot((act * h3).astype(_BF16), wdown_ref[...])
    o_ref[...] = _rmsnorm(out, fnw_ref[...])


def _resident(shape):
    return pl.BlockSpec(shape, lambda b, s: (0,) * len(shape), pipeline_mode=pl.Buffered(1))


def _token_tile(tile):
    return pl.BlockSpec((None, tile, D_MODEL), lambda b, s: (b, s, 0))


def _mixer_call(x, nw, win, caw, waout, scw, scb, dtb, alog, dskip, snw, wsout, wo, expand):
    batch, seq, _ = x.shape
    tm = MIXER_TILE
    params = (nw, win, caw, waout, scw, scb, dtb, alog, dskip, snw, wsout, wo, expand)
    return pl.pallas_call(
        _mixer_kernel,
        out_shape=jax.ShapeDtypeStruct(x.shape, _F32),
        grid=(batch, seq // tm),
        in_specs=[_token_tile(tm)] + [_resident(p.shape) for p in params],
        out_specs=_token_tile(tm),
        scratch_shapes=[
            pltpu.VMEM((tm, D_MODEL), _BF16),
            pltpu.VMEM((D_CONV // LANES, HALO + tm, LANES), _F32),
            pltpu.VMEM((D_CONV // LANES, tm, LANES), _F32),
            pltpu.VMEM((D_XBC // LANES, HALO + tm, LANES), _F32),
            pltpu.VMEM((D_INNER // LANES, tm, LANES), _F32),
            pltpu.VMEM((2 * N_GROUPS, tm, D_STATE), _F32),
            pltpu.VMEM((tm, D_INNER), _F32),
            pltpu.VMEM((tm, HEADS_PAD), _F32),
            pltpu.VMEM((CHUNK, HEADS_PAD), _F32),
            pltpu.VMEM((HEADS_PAD, CHUNK), _F32),
            pltpu.VMEM((tm, D_INNER), _BF16),
            pltpu.VMEM((tm, D_MODEL), _F32),
            pltpu.VMEM((D_STATE, D_INNER), _F32),
        ],
        compiler_params=pltpu.CompilerParams(
            dimension_semantics=("arbitrary", "arbitrary"),
            vmem_limit_bytes=MIXER_VMEM_LIMIT),
        name="mixer",
    )(x, *params)


def _ffn_call(h, nw, wup, fcw, fcb, wdown, fnw):
    batch, seq, _ = h.shape
    tm = FFN_TILE
    params = (nw, wup, fcw, fcb, wdown, fnw)
    return pl.pallas_call(
        _ffn_kernel,
        out_shape=jax.ShapeDtypeStruct(h.shape, _F32),
        grid=(batch, seq // tm),
        in_specs=[_token_tile(tm)] + [_resident(p.shape) for p in params],
        out_specs=_token_tile(tm),
        scratch_shapes=[
            pltpu.VMEM((D_FF // LANES, HALO + tm, LANES), _F32),
            pltpu.VMEM((D_FF // LANES, tm, LANES), _F32),
        ],
        compiler_params=pltpu.CompilerParams(
            dimension_semantics=("arbitrary", "arbitrary"),
            vmem_limit_bytes=FFN_VMEM_LIMIT),
        name="ffn",
    )(h, *params)


def _row(v, pad_to=None):
    v = v.astype(_F32).reshape(1, -1)
    if pad_to is not None:
        v = jnp.pad(v, ((0, 0), (0, pad_to - v.shape[1])))
    return v


def kernel(x, norm_mix_w, w_in, conv_a_w, w_a_out, ssd_conv_w, ssd_conv_b, dt_bias, a_log, d_skip,
           ssd_norm_w, w_s_out, w_o, norm_ffn_w, w_up, ffn_conv_w, ffn_conv_b, w_down, final_norm_w):
    depth = w_in.shape[0]
    assert x.shape[1] % MIXER_TILE == 0 and x.shape[1] % FFN_TILE == 0
    assert MIXER_TILE % CHUNK == 0
    assert MIXER_TILE % (CONV_STRIDE * SUBLANES) == 0 and FFN_TILE % (CONV_STRIDE * SUBLANES) == 0
    expand = (lax.broadcasted_iota(jnp.int32, (HEADS_PAD, D_INNER), 0)
              == lax.broadcasted_iota(jnp.int32, (HEADS_PAD, D_INNER), 1) // HEAD_DIM).astype(_BF16)
    h = x
    for l in range(depth):
        win = jnp.pad(w_in[l], ((0, 0), (0, N_IN_PAD - N_IN))).astype(_BF16)
        h = _mixer_call(
            h, _row(norm_mix_w[l]), win, conv_a_w[l].astype(_F32), w_a_out[l].astype(_BF16),
            ssd_conv_w[l].astype(_F32), _row(ssd_conv_b[l]), _row(dt_bias[l], HEADS_PAD),
            _row(a_log[l], HEADS_PAD), _row(jnp.repeat(d_skip[l], HEAD_DIM)), _row(ssd_norm_w[l]),
            w_s_out[l].astype(_BF16), w_o[l].astype(_BF16), expand)
        assert depth == 1
        h = _ffn_call(h, _row(norm_ffn_w[l]), w_up[l].astype(_BF16), ffn_conv_w[l].astype(_F32),
                      _row(ffn_conv_b[l]), w_down[l].astype(_BF16), _row(final_norm_w))
    return h
```

```python
import jax
import jax.numpy as jnp
from jax import lax
from jax.experimental import pallas as pl
from jax.experimental.pallas import tpu as pltpu

D_MODEL = 1024
D_CONV = D_MODEL
CONV_A_WIDTH = 3
D_INNER = 2048
HEAD_DIM = 64
N_HEADS = 32
N_GROUPS = 4
D_STATE = 128
SSD_CONV_WIDTH = 4
CHUNK = 128
D_XBC = D_INNER + 2 * N_GROUPS * D_STATE
D_FF = 2816
FFN_CONV_WIDTH = 3
EPS = 1e-5
LOG2_E = 1.4426950408889634

LANES = 128
SUBLANES = 8
CONV_STRIDE = 4

HEADS_PAD = LANES
OFF_GATE_A = 0
OFF_GATE_S = OFF_GATE_A + D_MODEL
OFF_B_A = OFF_GATE_S + D_MODEL
OFF_C_A = OFF_B_A + D_CONV
OFF_V_A = OFF_C_A + D_CONV
OFF_Z = OFF_V_A + D_CONV
OFF_XBC = OFF_Z + D_INNER
OFF_DT = OFF_XBC + D_XBC
N_IN = OFF_DT + N_HEADS

HEADS_PER_GROUP = N_HEADS // N_GROUPS
GROUP_WIDTH = HEADS_PER_GROUP * HEAD_DIM
GROUP_SLABS = GROUP_WIDTH // LANES
HALO = SUBLANES

MIXER_TILE = 256
FFN_TILE = 512
MIXER_VMEM_LIMIT = 56 * 1024 * 1024
FFN_VMEM_LIMIT = 48 * 1024 * 1024

_F32 = jnp.float32
_BF16 = jnp.bfloat16


def _dot(a, b):
    return jnp.dot(a, b, preferred_element_type=_F32)


def _rmsnorm(x, w):
    return x * lax.rsqrt(jnp.mean(x * x, axis=-1, keepdims=True) + EPS) * w


def _silu(x):
    return x * jax.nn.sigmoid(x)


def _to_slabs(buf_ref, value, row0):
    rows = value.shape[0]
    for s in range(value.shape[1] // LANES):
        buf_ref[s, row0:row0 + rows, :] = value[:, s * LANES:(s + 1) * LANES]


def _from_slabs(buf_ref, rows, slabs):
    return jnp.concatenate([buf_ref[s, rows, :] for s in slabs], axis=1)


def _causal_conv_slab(buf_ref, slab, w_ref, width, rows):
    n = rows // CONV_STRIDE
    first = HALO - (width - 1)
    reads = [buf_ref[slab, pl.ds(j, n, stride=CONV_STRIDE), :]
             for j in range(first, HALO + CONV_STRIDE)]
    cols = slice(slab * LANES, (slab + 1) * LANES)
    taps = [w_ref[k:k + 1, cols] for k in range(width)]
    outs = []
    for r in range(CONV_STRIDE):
        acc = taps[0] * reads[r]
        for k in range(1, width):
            acc = acc + taps[k] * reads[r + k]
        outs.append(acc)
    return outs


def _store_strided(dst_ref, slab, outs):
    n = outs[0].shape[0]
    for r, val in enumerate(outs):
        dst_ref[slab, pl.ds(r, n, stride=CONV_STRIDE), :] = val


def _keep_halo(buf_ref, rows):
    for s in range(buf_ref.shape[0]):
        buf_ref[s, 0:HALO, :] = buf_ref[s, rows:rows + HALO, :]


def _split3(x):
    hi = x.astype(_BF16)
    r1 = x - hi.astype(_F32)
    mid = r1.astype(_BF16)
    lo = (r1 - mid.astype(_F32)).astype(_BF16)
    return hi, mid, lo


def _mixer_kernel(x_ref, nw_ref, win_ref, wdt_ref, caw_ref, waout_ref, scw_ref, scb_ref, dtb_ref, alog_ref,
                  dskip_ref, snw_ref, wsout_ref, wo_ref, expand_ref,
                  o_ref,
                  u_ref, cv_ref, conva_ref, yain_ref, xbc_ref, xs_ref, bc_ref, z_ref, dt_ref, acum_ref,
                  acumt_ref, yn_ref, merged_ref, gs_ref, state_ref):
    tm = x_ref.shape[0]
    n_xs_slabs = D_INNER // LANES

    @pl.when(pl.program_id(1) == 0)
    def _():
        cv_ref[:, 0:HALO, :] = jnp.zeros((cv_ref.shape[0], HALO, LANES), _F32)
        xbc_ref[:, 0:HALO, :] = jnp.zeros((xbc_ref.shape[0], HALO, LANES), _F32)
        state_ref[...] = jnp.zeros(state_ref.shape, _F32)

    x = x_ref[...]
    u_ref[...] = _rmsnorm(x, nw_ref[...]).astype(_BF16)

    def proj(off, width):
        return _dot(u_ref[...], win_ref[:, off:off + width])

    _to_slabs(xbc_ref, proj(OFF_XBC, D_XBC), HALO)
    for s in range(D_XBC // LANES):
        cols = slice(s * LANES, (s + 1) * LANES)
        outs = [_silu(o + scb_ref[:, cols])
                for o in _causal_conv_slab(xbc_ref, s, scw_ref, SSD_CONV_WIDTH, tm)]
        if s < n_xs_slabs:
            _store_strided(xs_ref, s, outs)
        else:
            _store_strided(bc_ref, s - n_xs_slabs, outs)
    _keep_halo(xbc_ref, tm)
    z_ref[...] = _silu(proj(OFF_Z, D_INNER))
    dt_raw = _dot(u_ref[...], wdt_ref[...]) + dtb_ref[...]
    dt_ref[...] = jnp.maximum(dt_raw, 0.0) + jnp.log1p(jnp.exp(-jnp.abs(dt_raw)))

    a_slabs = range(D_CONV // LANES)

    def step_c():
        _to_slabs(conva_ref, proj(OFF_C_A, D_CONV), 0)

    def step_cv():
        _to_slabs(cv_ref, proj(OFF_V_A, D_CONV) * _from_slabs(conva_ref, slice(0, tm), a_slabs), HALO)

    def step_conv_b():
        for s in a_slabs:
            _store_strided(conva_ref, s, _causal_conv_slab(cv_ref, s, caw_ref, CONV_A_WIDTH, tm))
        _keep_halo(cv_ref, tm)
        conv_a = _from_slabs(conva_ref, slice(0, tm), a_slabs)
        yain_ref[...] = (proj(OFF_B_A, D_CONV) * conv_a).astype(_BF16)

    def step_ya():
        merged_ref[...] = _dot(yain_ref[...], waout_ref[...])

    def step_gate_a():
        merged_ref[...] = jax.nn.sigmoid(proj(OFF_GATE_A, D_MODEL)) * merged_ref[...]

    def step_gate_s():
        gs_ref[...] = jax.nn.sigmoid(proj(OFF_GATE_S, D_MODEL))

    dense_steps = [step_c, step_cv, step_conv_b, step_ya, step_gate_a, step_gate_s]

    a_row = -jnp.exp(alog_ref[...])
    row_id = lax.broadcasted_iota(jnp.int32, (CHUNK, CHUNK), 0)
    col_id = lax.broadcasted_iota(jnp.int32, (CHUNK, CHUNK), 1)
    causal = row_id >= col_id
    tril = causal.astype(_BF16)
    low_half = lax.broadcasted_iota(jnp.int32, (CHUNK, LANES), 1) < HEAD_DIM

    for ci in range(tm // CHUNK):
        rows = slice(ci * CHUNK, (ci + 1) * CHUNK)
        dt = dt_ref[rows, :]
        hi, mid, lo = _split3(dt * a_row)
        a_cum = _dot(tril, hi) + _dot(tril, mid) + _dot(tril, lo)
        a2 = a_cum * LOG2_E
        acum_ref[...] = a2
        acumt_ref[...] = (a2 - jnp.log(dt) * LOG2_E).T
        decay_to_end = jnp.exp2(a2[CHUNK - 1:CHUNK, :] - a2)
        decay_in = jnp.exp2(a2)
        per_head = jnp.concatenate([dt * decay_to_end, decay_in], axis=0).astype(_BF16)

        for g in range(N_GROUPS):
            gcols = slice(g * GROUP_WIDTH, (g + 1) * GROUP_WIDTH)
            per_chan = _dot(per_head, expand_ref[:, gcols])
            dtdecay_full = per_chan[0:CHUNK]
            decay_in_full = per_chan[CHUNK:2 * CHUNK]
            xs_g = _from_slabs(xs_ref, rows, range(g * GROUP_SLABS, (g + 1) * GROUP_SLABS))
            x_bf = xs_g.astype(_BF16)
            x_end = (xs_g * dtdecay_full).astype(_BF16)
            b_g = bc_ref[g, rows, :].astype(_BF16)
            c_g = bc_ref[N_GROUPS + g, rows, :].astype(_BF16)
            cb = lax.dot_general(c_g, b_g, (((1,), (1,)), ((), ())), preferred_element_type=_F32)
            state_prev = state_ref[:, gcols]
            y_off = _dot(c_g, state_prev.astype(_BF16))
            state_new = lax.dot_general(b_g, x_end, (((0,), (0,)), ((), ())),
                                        preferred_element_type=_F32)
            state_ref[:, gcols] = state_prev * decay_in_full[CHUNK - 1:CHUNK, :] + state_new

            y_cols = []
            for jp in range(GROUP_SLABS):
                h0 = g * HEADS_PER_GROUP + 2 * jp
                pcols = slice(h0 * HEAD_DIM, (h0 + 2) * HEAD_DIM)
                lcols = slice(jp * LANES, (jp + 1) * LANES)
                decays = []
                for h in (h0, h0 + 1):
                    seg = acum_ref[:, h:h + 1] - acumt_ref[h:h + 1, :]
                    decays.append((cb * jnp.exp2(jnp.where(causal, seg, -jnp.inf))).astype(_BF16))
                lhs = jnp.concatenate(decays, axis=1)
                x_pair = x_bf[:, lcols]
                zero = jnp.zeros_like(x_pair)
                rhs = jnp.concatenate([jnp.where(low_half, x_pair, zero),
                                       jnp.where(low_half, zero, x_pair)], axis=0)
                y_diag = _dot(lhs, rhs)
                y_cols.append(y_diag + y_off[:, lcols] * decay_in_full[:, lcols]
                              + dskip_ref[:, pcols] * xs_g[:, lcols])
            yz = jnp.concatenate(y_cols, axis=1) * z_ref[rows, gcols]
            yn = yz * lax.rsqrt(jnp.mean(yz * yz, axis=-1, keepdims=True) + EPS) * snw_ref[:, gcols]
            yn_ref[rows, gcols] = yn.astype(_BF16)
            if dense_steps:
                dense_steps.pop(0)()
    while dense_steps:
        dense_steps.pop(0)()

    y_s = _dot(yn_ref[...], wsout_ref[...])
    merged = merged_ref[...] + gs_ref[...] * y_s
    o_ref[...] = x_ref[...] + _dot(merged.astype(_BF16), wo_ref[...])


def _ffn_kernel(h_ref, nw_ref, wup_ref, fcw_ref, fcb_ref, wdown_ref, fnw_ref, o_ref, h1_ref, act_ref):
    tm = h_ref.shape[0]

    @pl.when(pl.program_id(1) == 0)
    def _():
        h1_ref[:, 0:HALO, :] = jnp.zeros((h1_ref.shape[0], HALO, LANES), _F32)

    h = h_ref[...]
    v = _rmsnorm(h, nw_ref[...]).astype(_BF16)
    _to_slabs(h1_ref, _dot(v, wup_ref[:, 0:D_FF]), HALO)
    h3 = _dot(v, wup_ref[:, D_FF:2 * D_FF])
    for s in range(D_FF // LANES):
        cols = slice(s * LANES, (s + 1) * LANES)
        outs = [_silu(o + fcb_ref[:, cols])
                for o in _causal_conv_slab(h1_ref, s, fcw_ref, FFN_CONV_WIDTH, tm)]
        _store_strided(act_ref, s, outs)
    _keep_halo(h1_ref, tm)
    act = _from_slabs(act_ref, slice(0, tm), range(D_FF // LANES))
    out = h + _dot((act * h3).astype(_BF16), wdown_ref[...])
    o_ref[...] = _rmsnorm(out, fnw_ref[...])


def _resident(shape):
    return pl.BlockSpec(shape, lambda b, s: (0,) * len(shape), pipeline_mode=pl.Buffered(1))


def _token_tile(tile):
    return pl.BlockSpec((None, tile, D_MODEL), lambda b, s: (b, s, 0))


def _mixer_call(x, nw, win, wdt, caw, waout, scw, scb, dtb, alog, dskip, snw, wsout, wo, expand):
    batch, seq, _ = x.shape
    tm = MIXER_TILE
    params = (nw, win, wdt, caw, waout, scw, scb, dtb, alog, dskip, snw, wsout, wo, expand)
    return pl.pallas_call(
        _mixer_kernel,
        out_shape=jax.ShapeDtypeStruct(x.shape, _F32),
        grid=(batch, seq // tm),
        in_specs=[_token_tile(tm)] + [_resident(p.shape) for p in params],
        out_specs=_token_tile(tm),
        scratch_shapes=[
            pltpu.VMEM((tm, D_MODEL), _BF16),
            pltpu.VMEM((D_CONV // LANES, HALO + tm, LANES), _F32),
            pltpu.VMEM((D_CONV // LANES, tm, LANES), _F32),
            pltpu.VMEM((tm, D_CONV), _BF16),
            pltpu.VMEM((D_XBC // LANES, HALO + tm, LANES), _F32),
            pltpu.VMEM((D_INNER // LANES, tm, LANES), _F32),
            pltpu.VMEM((2 * N_GROUPS, tm, D_STATE), _F32),
            pltpu.VMEM((tm, D_INNER), _F32),
            pltpu.VMEM((tm, HEADS_PAD), _F32),
            pltpu.VMEM((CHUNK, HEADS_PAD), _F32),
            pltpu.VMEM((HEADS_PAD, CHUNK), _F32),
            pltpu.VMEM((tm, D_INNER), _BF16),
            pltpu.VMEM((tm, D_MODEL), _F32),
            pltpu.VMEM((tm, D_MODEL), _F32),
            pltpu.VMEM((D_STATE, D_INNER), _F32),
        ],
        compiler_params=pltpu.CompilerParams(
            dimension_semantics=("arbitrary", "arbitrary"),
            vmem_limit_bytes=MIXER_VMEM_LIMIT),
        name="mixer",
    )(x, *params)


def _ffn_call(h, nw, wup, fcw, fcb, wdown, fnw):
    batch, seq, _ = h.shape
    tm = FFN_TILE
    params = (nw, wup, fcw, fcb, wdown, fnw)
    return pl.pallas_call(
        _ffn_kernel,
        out_shape=jax.ShapeDtypeStruct(h.shape, _F32),
        grid=(batch, seq // tm),
        in_specs=[_token_tile(tm)] + [_resident(p.shape) for p in params],
        out_specs=_token_tile(tm),
        scratch_shapes=[
            pltpu.VMEM((D_FF // LANES, HALO + tm, LANES), _F32),
            pltpu.VMEM((D_FF // LANES, tm, LANES), _F32),
        ],
        compiler_params=pltpu.CompilerParams(
            dimension_semantics=("arbitrary", "arbitrary"),
            vmem_limit_bytes=FFN_VMEM_LIMIT),
        name="ffn",
    )(h, *params)


def _row(v, pad_to=None):
    v = v.astype(_F32).reshape(1, -1)
    if pad_to is not None:
        v = jnp.pad(v, ((0, 0), (0, pad_to - v.shape[1])))
    return v


def kernel(x, norm_mix_w, w_in, conv_a_w, w_a_out, ssd_conv_w, ssd_conv_b, dt_bias, a_log, d_skip,
           ssd_norm_w, w_s_out, w_o, norm_ffn_w, w_up, ffn_conv_w, ffn_conv_b, w_down, final_norm_w):
    depth = w_in.shape[0]
    assert x.shape[1] % MIXER_TILE == 0 and x.shape[1] % FFN_TILE == 0
    assert MIXER_TILE % CHUNK == 0
    assert MIXER_TILE % (CONV_STRIDE * SUBLANES) == 0 and FFN_TILE % (CONV_STRIDE * SUBLANES) == 0
    expand = (lax.broadcasted_iota(jnp.int32, (HEADS_PAD, D_INNER), 0)
              == lax.broadcasted_iota(jnp.int32, (HEADS_PAD, D_INNER), 1) // HEAD_DIM).astype(_BF16)
    h = x
    for l in range(depth):
        assert w_in.shape[2] == N_IN
        win = w_in[l, :, 0:OFF_DT].astype(_BF16)
        wdt = jnp.pad(w_in[l, :, OFF_DT:N_IN], ((0, 0), (0, HEADS_PAD - N_HEADS))).astype(_BF16)
        h = _mixer_call(
            h, _row(norm_mix_w[l]), win, wdt, conv_a_w[l].astype(_F32), w_a_out[l].astype(_BF16),
            ssd_conv_w[l].astype(_F32), _row(ssd_conv_b[l]), _row(dt_bias[l], HEADS_PAD),
            _row(a_log[l], HEADS_PAD), _row(jnp.repeat(d_skip[l], HEAD_DIM)), _row(ssd_norm_w[l]),
            w_s_out[l].astype(_BF16), w_o[l].astype(_BF16), expand)
        assert depth == 1
        h = _ffn_call(h, _row(norm_ffn_w[l]), w_up[l].astype(_BF16), ffn_conv_w[l].astype(_F32),
                      _row(ffn_conv_b[l]), w_down[l].astype(_BF16), _row(final_norm_w))
    return h
```

```python
import functools

import jax
import jax.numpy as jnp
from jax import lax
from jax.experimental import pallas as pl
from jax.experimental.pallas import tpu as pltpu

D_MODEL = 1024
D_CONV = D_MODEL
CONV_A_WIDTH = 3
D_INNER = 2048
HEAD_DIM = 64
N_HEADS = 32
N_GROUPS = 4
D_STATE = 128
SSD_CONV_WIDTH = 4
CHUNK = 128
D_XBC = D_INNER + 2 * N_GROUPS * D_STATE
D_FF = 2816
FFN_CONV_WIDTH = 3
EPS = 1e-5
LOG2_E = 1.4426950408889634

LANES = 128
SUBLANES = 8
CONV_STRIDE = 4

HEADS_PAD = LANES
OFF_GATE_A = 0
OFF_GATE_S = OFF_GATE_A + D_MODEL
OFF_B_A = OFF_GATE_S + D_MODEL
OFF_C_A = OFF_B_A + D_CONV
OFF_V_A = OFF_C_A + D_CONV
OFF_Z = OFF_V_A + D_CONV
OFF_XBC = OFF_Z + D_INNER
OFF_DT = OFF_XBC + D_XBC
N_IN = OFF_DT + N_HEADS

HEADS_PER_GROUP = N_HEADS // N_GROUPS
GROUP_WIDTH = HEADS_PER_GROUP * HEAD_DIM
GROUP_SLABS = GROUP_WIDTH // LANES
HALO = SUBLANES

MIXER_TILE = 256
FFN_TILE = 512
WIDE_STAGE_ROWS = 32
UP_STAGE_ROWS = 64
SQUARE_STAGE_ROWS = 256
MIXER_VMEM_LIMIT = 60 * 1024 * 1024
FFN_VMEM_LIMIT = 52 * 1024 * 1024

_F32 = jnp.float32
_BF16 = jnp.bfloat16


def _dot(a, b):
    return jnp.dot(a, b, preferred_element_type=_F32)


def _rmsnorm(x, w):
    return x * lax.rsqrt(jnp.mean(x * x, axis=-1, keepdims=True) + EPS) * w


def _silu(x):
    return x * jax.nn.sigmoid(x)


def _to_slabs(buf_ref, value, row0):
    rows = value.shape[0]
    for s in range(value.shape[1] // LANES):
        buf_ref[s, row0:row0 + rows, :] = value[:, s * LANES:(s + 1) * LANES]


def _from_slabs(buf_ref, rows, slabs):
    return jnp.concatenate([buf_ref[s, rows, :] for s in slabs], axis=1)


def _causal_conv_slab(buf_ref, slab, w_ref, width, rows):
    n = rows // CONV_STRIDE
    first = HALO - (width - 1)
    reads = [buf_ref[slab, pl.ds(j, n, stride=CONV_STRIDE), :]
             for j in range(first, HALO + CONV_STRIDE)]
    cols = slice(slab * LANES, (slab + 1) * LANES)
    taps = [w_ref[k:k + 1, cols] for k in range(width)]
    outs = []
    for r in range(CONV_STRIDE):
        acc = taps[0] * reads[r]
        for k in range(1, width):
            acc = acc + taps[k] * reads[r + k]
        outs.append(acc)
    return outs


def _store_strided(dst_ref, slab, outs):
    n = outs[0].shape[0]
    for r, val in enumerate(outs):
        dst_ref[slab, pl.ds(r, n, stride=CONV_STRIDE), :] = val


def _keep_halo(buf_ref, rows):
    for s in range(buf_ref.shape[0]):
        buf_ref[s, 0:HALO, :] = buf_ref[s, rows:rows + HALO, :]


def _stream_cast(src_hbm, dst_ref, stage_ref, sem_ref):
    block_rows = stage_ref.shape[1]
    n_blocks = dst_ref.shape[0] // block_rows

    def block_copy(i, slot):
        return pltpu.make_async_copy(src_hbm.at[pl.ds(i * block_rows, block_rows), :],
                                     stage_ref.at[slot], sem_ref.at[slot])

    block_copy(0, 0).start()

    def body(i, carry):
        slot = lax.rem(i, 2)

        @pl.when(i + 1 < n_blocks)
        def _():
            block_copy(i + 1, 1 - slot).start()

        block_copy(i, slot).wait()
        r0 = pl.multiple_of(i * block_rows, block_rows)
        dst_ref[pl.ds(r0, block_rows), :] = stage_ref[slot].astype(_BF16)
        return carry

    lax.fori_loop(0, n_blocks, body, 0)


def _first_step():
    return jnp.logical_and(pl.program_id(0) == 0, pl.program_id(1) == 0)


def _split3(x):
    hi = x.astype(_BF16)
    r1 = x - hi.astype(_F32)
    mid = r1.astype(_BF16)
    lo = (r1 - mid.astype(_F32)).astype(_BF16)
    return hi, mid, lo


def _mixer_kernel(x_ref, nw_ref, win_hbm, wdt_ref, caw_ref, waout_hbm, scw_ref, scb_ref, dtb_ref, alog_ref,
                  dskip_ref, snw_ref, wsout_hbm, wo_hbm,
                  o_ref,
                  win_ref, waout_ref, wsout_ref, wo_ref, expand_ref, wide_stage_ref, square_stage_ref,
                  stage_sem,
                  u_ref, cv_ref, conva_ref, yain_ref, xbc_ref, xs_ref, bc_ref, z_ref, dt_ref, acum_ref,
                  acumt_ref, yn_ref, merged_ref, gs_ref, state_ref):
    tm = x_ref.shape[0]
    n_xs_slabs = D_INNER // LANES

    @pl.when(_first_step())
    def _():
        _stream_cast(win_hbm.at[:, pl.ds(0, OFF_DT)], win_ref, wide_stage_ref, stage_sem)
        _stream_cast(waout_hbm, waout_ref, square_stage_ref, stage_sem)
        _stream_cast(wsout_hbm, wsout_ref, square_stage_ref, stage_sem)
        _stream_cast(wo_hbm, wo_ref, square_stage_ref, stage_sem)
        head = lax.broadcasted_iota(jnp.int32, (HEADS_PAD, D_INNER), 0)
        chan = lax.broadcasted_iota(jnp.int32, (HEADS_PAD, D_INNER), 1)
        owner = lax.shift_right_logical(chan, HEAD_DIM.bit_length() - 1)
        expand_ref[...] = jnp.where(head == owner, 1.0, 0.0).astype(_BF16)

    @pl.when(pl.program_id(1) == 0)
    def _():
        cv_ref[:, 0:HALO, :] = jnp.zeros((cv_ref.shape[0], HALO, LANES), _F32)
        xbc_ref[:, 0:HALO, :] = jnp.zeros((xbc_ref.shape[0], HALO, LANES), _F32)
        state_ref[...] = jnp.zeros(state_ref.shape, _F32)

    x = x_ref[...]
    u_ref[...] = _rmsnorm(x, nw_ref[...]).astype(_BF16)

    def proj(off, width):
        return _dot(u_ref[...], win_ref[:, off:off + width])

    _to_slabs(xbc_ref, proj(OFF_XBC, D_XBC), HALO)
    for s in range(D_XBC // LANES):
        cols = slice(s * LANES, (s + 1) * LANES)
        outs = [_silu(o + scb_ref[:, cols])
                for o in _causal_conv_slab(xbc_ref, s, scw_ref, SSD_CONV_WIDTH, tm)]
        if s < n_xs_slabs:
            _store_strided(xs_ref, s, outs)
        else:
            _store_strided(bc_ref, s - n_xs_slabs, outs)
    _keep_halo(xbc_ref, tm)
    z_ref[...] = _silu(proj(OFF_Z, D_INNER))
    dt_raw = _dot(u_ref[...], wdt_ref[...]) + dtb_ref[...]
    dt_ref[...] = jnp.maximum(dt_raw, 0.0) + jnp.log1p(jnp.exp(-jnp.abs(dt_raw)))

    a_slabs = range(D_CONV // LANES)

    def step_c():
        _to_slabs(conva_ref, proj(OFF_C_A, D_CONV), 0)

    def step_cv():
        _to_slabs(cv_ref, proj(OFF_V_A, D_CONV) * _from_slabs(conva_ref, slice(0, tm), a_slabs), HALO)

    def step_conv_b():
        for s in a_slabs:
            _store_strided(conva_ref, s, _causal_conv_slab(cv_ref, s, caw_ref, CONV_A_WIDTH, tm))
        _keep_halo(cv_ref, tm)
        conv_a = _from_slabs(conva_ref, slice(0, tm), a_slabs)
        yain_ref[...] = (proj(OFF_B_A, D_CONV) * conv_a).astype(_BF16)

    def step_ya():
        merged_ref[...] = _dot(yain_ref[...], waout_ref[...])

    def step_gate_a():
        merged_ref[...] = jax.nn.sigmoid(proj(OFF_GATE_A, D_MODEL)) * merged_ref[...]

    def step_gate_s():
        gs_ref[...] = jax.nn.sigmoid(proj(OFF_GATE_S, D_MODEL))

    def step_out(rows):
        y_s = _dot(yn_ref[rows, :], wsout_ref[...])
        merged = merged_ref[rows, :] + gs_ref[rows, :] * y_s
        o_ref[rows, :] = x_ref[rows, :] + _dot(merged.astype(_BF16), wo_ref[...])

    dense_steps = [step_c, step_cv, step_conv_b, step_ya, step_gate_a, step_gate_s]

    a_row = -jnp.exp(alog_ref[...])
    row_id = lax.broadcasted_iota(jnp.int32, (CHUNK, CHUNK), 0)
    col_id = lax.broadcasted_iota(jnp.int32, (CHUNK, CHUNK), 1)
    causal = row_id >= col_id
    tril = causal.astype(_BF16)
    low_half = lax.broadcasted_iota(jnp.int32, (CHUNK, LANES), 1) < HEAD_DIM

    for ci in range(tm // CHUNK):
        rows = slice(ci * CHUNK, (ci + 1) * CHUNK)
        dt = dt_ref[rows, :]
        hi, mid, lo = _split3(dt * a_row)
        a_cum = _dot(tril, hi) + _dot(tril, mid) + _dot(tril, lo)
        a2 = a_cum * LOG2_E
        acum_ref[...] = a2
        acumt_ref[...] = (a2 - jnp.log(dt) * LOG2_E).T
        decay_to_end = jnp.exp2(a2[CHUNK - 1:CHUNK, :] - a2)
        decay_in = jnp.exp2(a2)
        per_head = jnp.concatenate([dt * decay_to_end, decay_in], axis=0).astype(_BF16)

        for g in range(N_GROUPS):
            gcols = slice(g * GROUP_WIDTH, (g + 1) * GROUP_WIDTH)
            per_chan = _dot(per_head, expand_ref[:, gcols])
            dtdecay_full = per_chan[0:CHUNK]
            decay_in_full = per_chan[CHUNK:2 * CHUNK]
            xs_g = _from_slabs(xs_ref, rows, range(g * GROUP_SLABS, (g + 1) * GROUP_SLABS))
            x_bf = xs_g.astype(_BF16)
            x_end = (xs_g * dtdecay_full).astype(_BF16)
            b_g = bc_ref[g, rows, :].astype(_BF16)
            c_g = bc_ref[N_GROUPS + g, rows, :].astype(_BF16)
            cb = lax.dot_general(c_g, b_g, (((1,), (1,)), ((), ())), preferred_element_type=_F32)
            state_prev = state_ref[:, gcols]
            y_off = _dot(c_g, state_prev.astype(_BF16))
            state_new = lax.dot_general(b_g, x_end, (((0,), (0,)), ((), ())),
                                        preferred_element_type=_F32)
            state_ref[:, gcols] = state_prev * decay_in_full[CHUNK - 1:CHUNK, :] + state_new

            y_cols = []
            for jp in range(GROUP_SLABS):
                h0 = g * HEADS_PER_GROUP + 2 * jp
                pcols = slice(h0 * HEAD_DIM, (h0 + 2) * HEAD_DIM)
                lcols = slice(jp * LANES, (jp + 1) * LANES)
                decays = []
                for h in (h0, h0 + 1):
                    seg = acum_ref[:, h:h + 1] - acumt_ref[h:h + 1, :]
                    decays.append((cb * jnp.exp2(jnp.where(causal, seg, -jnp.inf))).astype(_BF16))
                lhs = jnp.concatenate(decays, axis=1)
                x_pair = x_bf[:, lcols]
                zero = jnp.zeros_like(x_pair)
                rhs = jnp.concatenate([jnp.where(low_half, x_pair, zero),
                                       jnp.where(low_half, zero, x_pair)], axis=0)
                y_diag = _dot(lhs, rhs)
                y_cols.append(y_diag + y_off[:, lcols] * decay_in_full[:, lcols]
                              + dskip_ref[:, pcols] * xs_g[:, lcols])
            yz = jnp.concatenate(y_cols, axis=1) * z_ref[rows, gcols]
            yn = yz * lax.rsqrt(jnp.mean(yz * yz, axis=-1, keepdims=True) + EPS) * snw_ref[:, gcols]
            yn_ref[rows, gcols] = yn.astype(_BF16)
            if dense_steps:
                dense_steps.pop(0)()
        dense_steps.append(functools.partial(step_out, rows))
    while dense_steps:
        dense_steps.pop(0)()


def _ffn_kernel(h_ref, nw_ref, wup_hbm, fcw_ref, fcb_ref, wdown_hbm, fnw_ref, o_ref,
                wup_ref, wdown_ref, up_stage_ref, square_stage_ref, stage_sem, h1_ref, act_ref):
    tm = h_ref.shape[0]

    @pl.when(_first_step())
    def _():
        _stream_cast(wup_hbm, wup_ref, up_stage_ref, stage_sem)
        _stream_cast(wdown_hbm, wdown_ref, square_stage_ref, stage_sem)

    @pl.when(pl.program_id(1) == 0)
    def _():
        h1_ref[:, 0:HALO, :] = jnp.zeros((h1_ref.shape[0], HALO, LANES), _F32)

    h = h_ref[...]
    v = _rmsnorm(h, nw_ref[...]).astype(_BF16)
    _to_slabs(h1_ref, _dot(v, wup_ref[:, 0:D_FF]), HALO)
    h3 = _dot(v, wup_ref[:, D_FF:2 * D_FF])
    for s in range(D_FF // LANES):
        cols = slice(s * LANES, (s + 1) * LANES)
        outs = [_silu(o + fcb_ref[:, cols])
                for o in _causal_conv_slab(h1_ref, s, fcw_ref, FFN_CONV_WIDTH, tm)]
        _store_strided(act_ref, s, outs)
    _keep_halo(h1_ref, tm)
    act = _from_slabs(act_ref, slice(0, tm), range(D_FF // LANES))
    out = h + _dot((act * h3).astype(_BF16), wdown_ref[...])
    o_ref[...] = _rmsnorm(out, fnw_ref[...])


def _resident(shape):
    return pl.BlockSpec(shape, lambda b, s: (0,) * len(shape), pipeline_mode=pl.Buffered(1))


def _token_tile(tile):
    return pl.BlockSpec((None, tile, D_MODEL), lambda b, s: (b, s, 0))


_IN_HBM = pl.BlockSpec(memory_space=pl.ANY)


def _mixer_call(x, nw, win, wdt, caw, waout, scw, scb, dtb, alog, dskip, snw, wsout, wo):
    batch, seq, _ = x.shape
    tm = MIXER_TILE
    params = (nw, win, wdt, caw, waout, scw, scb, dtb, alog, dskip, snw, wsout, wo)
    in_hbm = (win, waout, wsout, wo)
    assert win.shape == (D_MODEL, N_IN) and D_MODEL % WIDE_STAGE_ROWS == 0
    assert all(w.shape[0] % SQUARE_STAGE_ROWS == 0 and w.shape[1] == D_MODEL for w in (waout, wsout, wo))
    return pl.pallas_call(
        _mixer_kernel,
        out_shape=jax.ShapeDtypeStruct(x.shape, _F32),
        grid=(batch, seq // tm),
        in_specs=[_token_tile(tm)] + [_IN_HBM if any(p is w for w in in_hbm) else _resident(p.shape)
                                      for p in params],
        out_specs=_token_tile(tm),
        scratch_shapes=[
            pltpu.VMEM((D_MODEL, OFF_DT), _BF16),
            pltpu.VMEM(waout.shape, _BF16),
            pltpu.VMEM(wsout.shape, _BF16),
            pltpu.VMEM(wo.shape, _BF16),
            pltpu.VMEM((HEADS_PAD, D_INNER), _BF16),
            pltpu.VMEM((2, WIDE_STAGE_ROWS, OFF_DT), _F32),
            pltpu.VMEM((2, SQUARE_STAGE_ROWS, D_MODEL), _F32),
            pltpu.SemaphoreType.DMA((2,)),
            pltpu.VMEM((tm, D_MODEL), _BF16),
            pltpu.VMEM((D_CONV // LANES, HALO + tm, LANES), _F32),
            pltpu.VMEM((D_CONV // LANES, tm, LANES), _F32),
            pltpu.VMEM((tm, D_CONV), _BF16),
            pltpu.VMEM((D_XBC // LANES, HALO + tm, LANES), _F32),
            pltpu.VMEM((D_INNER // LANES, tm, LANES), _F32),
            pltpu.VMEM((2 * N_GROUPS, tm, D_STATE), _F32),
            pltpu.VMEM((tm, D_INNER), _F32),
            pltpu.VMEM((tm, HEADS_PAD), _F32),
            pltpu.VMEM((CHUNK, HEADS_PAD), _F32),
            pltpu.VMEM((HEADS_PAD, CHUNK), _F32),
            pltpu.VMEM((tm, D_INNER), _BF16),
            pltpu.VMEM((tm, D_MODEL), _F32),
            pltpu.VMEM((tm, D_MODEL), _F32),
            pltpu.VMEM((D_STATE, D_INNER), _F32),
        ],
        compiler_params=pltpu.CompilerParams(
            dimension_semantics=("arbitrary", "arbitrary"),
            vmem_limit_bytes=MIXER_VMEM_LIMIT),
        name="mixer",
    )(x, *params)


def _ffn_call(h, nw, wup, fcw, fcb, wdown, fnw):
    batch, seq, _ = h.shape
    tm = FFN_TILE
    params = (nw, wup, fcw, fcb, wdown, fnw)
    in_hbm = (wup, wdown)
    assert wup.shape == (D_MODEL, 2 * D_FF) and D_MODEL % UP_STAGE_ROWS == 0
    assert wdown.shape == (D_FF, D_MODEL) and D_FF % SQUARE_STAGE_ROWS == 0
    return pl.pallas_call(
        _ffn_kernel,
        out_shape=jax.ShapeDtypeStruct(h.shape, _F32),
        grid=(batch, seq // tm),
        in_specs=[_token_tile(tm)] + [_IN_HBM if any(p is w for w in in_hbm) else _resident(p.shape)
                                      for p in params],
        out_specs=_token_tile(tm),
        scratch_shapes=[
            pltpu.VMEM(wup.shape, _BF16),
            pltpu.VMEM(wdown.shape, _BF16),
            pltpu.VMEM((2, UP_STAGE_ROWS, 2 * D_FF), _F32),
            pltpu.VMEM((2, SQUARE_STAGE_ROWS, D_MODEL), _F32),
            pltpu.SemaphoreType.DMA((2,)),
            pltpu.VMEM((D_FF // LANES, HALO + tm, LANES), _F32),
            pltpu.VMEM((D_FF // LANES, tm, LANES), _F32),
        ],
        compiler_params=pltpu.CompilerParams(
            dimension_semantics=("arbitrary", "arbitrary"),
            vmem_limit_bytes=FFN_VMEM_LIMIT),
        name="ffn",
    )(h, *params)


def _row(v, pad_to=None):
    v = v.astype(_F32).reshape(1, -1)
    if pad_to is not None:
        v = jnp.pad(v, ((0, 0), (0, pad_to - v.shape[1])))
    return v


def kernel(x, norm_mix_w, w_in, conv_a_w, w_a_out, ssd_conv_w, ssd_conv_b, dt_bias, a_log, d_skip,
           ssd_norm_w, w_s_out, w_o, norm_ffn_w, w_up, ffn_conv_w, ffn_conv_b, w_down, final_norm_w):
    depth = w_in.shape[0]
    assert x.shape[1] % MIXER_TILE == 0 and x.shape[1] % FFN_TILE == 0
    assert MIXER_TILE % CHUNK == 0
    assert MIXER_TILE % (CONV_STRIDE * SUBLANES) == 0 and FFN_TILE % (CONV_STRIDE * SUBLANES) == 0
    assert HEAD_DIM & (HEAD_DIM - 1) == 0
    h = x
    for l in range(depth):
        wdt = jnp.pad(w_in[l, :, OFF_DT:N_IN], ((0, 0), (0, HEADS_PAD - N_HEADS))).astype(_BF16)
        h = _mixer_call(
            h, _row(norm_mix_w[l]), w_in[l].astype(_F32), wdt, conv_a_w[l].astype(_F32),
            w_a_out[l].astype(_F32), ssd_conv_w[l].astype(_F32), _row(ssd_conv_b[l]),
            _row(dt_bias[l], HEADS_PAD), _row(a_log[l], HEADS_PAD), _row(jnp.repeat(d_skip[l], HEAD_DIM)),
            _row(ssd_norm_w[l]), w_s_out[l].astype(_F32), w_o[l].astype(_F32))
        assert depth == 1
        h = _ffn_call(h, _row(norm_ffn_w[l]), w_up[l].astype(_F32), ffn_conv_w[l].astype(_F32),
                      _row(ffn_conv_b[l]), w_down[l].astype(_F32), _row(final_norm_w))
    return h
```

```python
import functools

import jax
import jax.numpy as jnp
from jax import lax
from jax.experimental import pallas as pl
from jax.experimental.pallas import tpu as pltpu

D_MODEL = 1024
D_CONV = D_MODEL
CONV_A_WIDTH = 3
D_INNER = 2048
HEAD_DIM = 64
N_HEADS = 32
N_GROUPS = 4
D_STATE = 128
SSD_CONV_WIDTH = 4
CHUNK = 128
D_XBC = D_INNER + 2 * N_GROUPS * D_STATE
D_FF = 2816
FFN_CONV_WIDTH = 3
EPS = 1e-5
LOG2_E = 1.4426950408889634

LANES = 128
SUBLANES = 8
CONV_STRIDE = 4

HEADS_PAD = LANES
OFF_GATE_A = 0
OFF_GATE_S = OFF_GATE_A + D_MODEL
OFF_B_A = OFF_GATE_S + D_MODEL
OFF_C_A = OFF_B_A + D_CONV
OFF_V_A = OFF_C_A + D_CONV
OFF_Z = OFF_V_A + D_CONV
OFF_XBC = OFF_Z + D_INNER
OFF_DT = OFF_XBC + D_XBC
N_IN = OFF_DT + N_HEADS

HEADS_PER_GROUP = N_HEADS // N_GROUPS
GROUP_WIDTH = HEADS_PER_GROUP * HEAD_DIM
GROUP_SLABS = GROUP_WIDTH // LANES
HALO = SUBLANES

MIXER_TILE = 256
FFN_TILE = 512
UP_STAGE_ROWS = 64
SQUARE_STAGE_ROWS = 256
PROJ_BLOCK = 256
MIXER_VMEM_LIMIT = 60 * 1024 * 1024
FFN_VMEM_LIMIT = 52 * 1024 * 1024

_F32 = jnp.float32
_BF16 = jnp.bfloat16


def _dot(a, b):
    return jnp.dot(a, b, preferred_element_type=_F32)


def _rmsnorm(x, w):
    return x * lax.rsqrt(jnp.mean(x * x, axis=-1, keepdims=True) + EPS) * w


def _silu(x):
    return x * jax.nn.sigmoid(x)


def _to_slabs(buf_ref, value, row0):
    rows = value.shape[0]
    for s in range(value.shape[1] // LANES):
        buf_ref[s, row0:row0 + rows, :] = value[:, s * LANES:(s + 1) * LANES]


def _from_slabs(buf_ref, rows, slabs):
    return jnp.concatenate([buf_ref[s, rows, :] for s in slabs], axis=1)


def _causal_conv_slab(buf_ref, slab, w_ref, width, rows):
    n = rows // CONV_STRIDE
    first = HALO - (width - 1)
    reads = [buf_ref[slab, pl.ds(j, n, stride=CONV_STRIDE), :]
             for j in range(first, HALO + CONV_STRIDE)]
    cols = slice(slab * LANES, (slab + 1) * LANES)
    taps = [w_ref[k:k + 1, cols] for k in range(width)]
    outs = []
    for r in range(CONV_STRIDE):
        acc = taps[0] * reads[r]
        for k in range(1, width):
            acc = acc + taps[k] * reads[r + k]
        outs.append(acc)
    return outs


def _store_strided(dst_ref, slab, outs):
    n = outs[0].shape[0]
    for r, val in enumerate(outs):
        dst_ref[slab, pl.ds(r, n, stride=CONV_STRIDE), :] = val


def _keep_halo(buf_ref, rows):
    for s in range(buf_ref.shape[0]):
        buf_ref[s, 0:HALO, :] = buf_ref[s, rows:rows + HALO, :]


def _stream_cast(src_hbm, dst_ref, stage_ref, sem_ref):
    block_rows = stage_ref.shape[1]
    n_blocks = dst_ref.shape[0] // block_rows

    def block_copy(i, slot):
        return pltpu.make_async_copy(src_hbm.at[pl.ds(i * block_rows, block_rows), :],
                                     stage_ref.at[slot], sem_ref.at[slot])

    block_copy(0, 0).start()

    def body(i, carry):
        slot = lax.rem(i, 2)

        @pl.when(i + 1 < n_blocks)
        def _():
            block_copy(i + 1, 1 - slot).start()

        block_copy(i, slot).wait()
        r0 = pl.multiple_of(i * block_rows, block_rows)
        dst_ref[pl.ds(r0, block_rows), :] = stage_ref[slot].astype(_BF16)
        return carry

    lax.fori_loop(0, n_blocks, body, 0)


def _stream_cast_transposed(src_hbm, dst_ref, stage_ref, sem_ref):
    block_rows = stage_ref.shape[1]
    n_blocks = dst_ref.shape[0]

    def block_copy(i, slot):
        return pltpu.make_async_copy(src_hbm.at[pl.ds(i * block_rows, block_rows), :],
                                     stage_ref.at[slot], sem_ref.at[slot])

    block_copy(0, 0).start()

    def body(i, carry):
        slot = lax.rem(i, 2)

        @pl.when(i + 1 < n_blocks)
        def _():
            block_copy(i + 1, 1 - slot).start()

        block_copy(i, slot).wait()
        dst_ref[i] = stage_ref[slot].T.astype(_BF16)
        return carry

    lax.fori_loop(0, n_blocks, body, 0)


def _first_step():
    return jnp.logical_and(pl.program_id(0) == 0, pl.program_id(1) == 0)


def _split3(x):
    hi = x.astype(_BF16)
    r1 = x - hi.astype(_F32)
    mid = r1.astype(_BF16)
    lo = (r1 - mid.astype(_F32)).astype(_BF16)
    return hi, mid, lo


def _mixer_kernel(x_ref, nw_ref, win_hbm, wdt_ref, caw_ref, waout_hbm, scw_ref, scb_ref, dtb_ref, alog_ref,
                  dskip_ref, snw_ref, wsout_hbm, wo_hbm,
                  o_ref,
                  win_ref, waout_ref, wsout_ref, wo_ref, expand_ref, square_stage_ref, stage_sem,
                  u_ref, cv_ref, conva_ref, yain_ref, xbc_ref, xs_ref, bc_ref, z_ref, acum_ref,
                  acumt_ref, perhead_ref, yn_ref, merged_ref, gs_ref, state_ref):
    tm = x_ref.shape[0]
    n_chunks = tm // CHUNK
    n_xs_slabs = D_INNER // LANES

    @pl.when(_first_step())
    def _():
        _stream_cast_transposed(win_hbm, win_ref, square_stage_ref, stage_sem)
        _stream_cast(waout_hbm, waout_ref, square_stage_ref, stage_sem)
        _stream_cast(wsout_hbm, wsout_ref, square_stage_ref, stage_sem)
        _stream_cast(wo_hbm, wo_ref, square_stage_ref, stage_sem)
        head = lax.broadcasted_iota(jnp.int32, (HEADS_PAD, D_INNER), 0)
        chan = lax.broadcasted_iota(jnp.int32, (HEADS_PAD, D_INNER), 1)
        owner = lax.shift_right_logical(chan, HEAD_DIM.bit_length() - 1)
        expand_ref[...] = jnp.where(head == owner, 1.0, 0.0).astype(_BF16)

    @pl.when(pl.program_id(1) == 0)
    def _():
        cv_ref[:, 0:HALO, :] = jnp.zeros((cv_ref.shape[0], HALO, LANES), _F32)
        xbc_ref[:, 0:HALO, :] = jnp.zeros((xbc_ref.shape[0], HALO, LANES), _F32)
        state_ref[...] = jnp.zeros(state_ref.shape, _F32)

    x = x_ref[...]
    u_ref[...] = _rmsnorm(x, nw_ref[...]).astype(_BF16)

    def proj(off, width):
        blocks = range(off // PROJ_BLOCK, (off + width) // PROJ_BLOCK)
        return jnp.concatenate([_dot(u_ref[...], win_ref[j]) for j in blocks], axis=1)

    a_row = -jnp.exp(alog_ref[...])
    row_id = lax.broadcasted_iota(jnp.int32, (CHUNK, CHUNK), 0)
    col_id = lax.broadcasted_iota(jnp.int32, (CHUNK, CHUNK), 1)
    causal = row_id >= col_id
    tril = causal.astype(_BF16)
    low_half = lax.broadcasted_iota(jnp.int32, (CHUNK, LANES), 1) < HEAD_DIM

    _to_slabs(xbc_ref, proj(OFF_XBC, D_XBC), HALO)

    dt_raw = _dot(u_ref[...], wdt_ref[...]) + dtb_ref[...]
    z_ref[...] = _silu(proj(OFF_Z, D_INNER))
    dt_all =jnp.maximum(dt_raw, 0.0) + jnp.log1p(jnp.exp(-jnp.abs(dt_raw)))
    for ci in range(n_chunks):
        dt = dt_all[ci * CHUNK:(ci + 1) * CHUNK, :]
        hi, mid, lo = _split3(dt * a_row)
        a_cum = _dot(tril, hi) + _dot(tril, mid) + _dot(tril, lo)
        a2 = a_cum * LOG2_E
        acum_ref[ci] = a2
        acumt_ref[ci] = (a2 - jnp.log(dt) * LOG2_E).T
        decay_to_end = jnp.exp2(a2[CHUNK - 1:CHUNK, :] - a2)
        decay_in = jnp.exp2(a2)
        perhead_ref[ci] = jnp.concatenate([dt * decay_to_end, decay_in], axis=0).astype(_BF16)

    for s in range(D_XBC // LANES):
        cols = slice(s * LANES, (s + 1) * LANES)
        outs = [_silu(o + scb_ref[:, cols])
                for o in _causal_conv_slab(xbc_ref, s, scw_ref, SSD_CONV_WIDTH, tm)]
        if s < n_xs_slabs:
            _store_strided(xs_ref, s, outs)
        else:
            _store_strided(bc_ref, s - n_xs_slabs, outs)
    _keep_halo(xbc_ref, tm)

    a_slabs = range(D_CONV // LANES)

    def step_c():
        _to_slabs(conva_ref, proj(OFF_C_A, D_CONV), 0)

    def step_cv():
        _to_slabs(cv_ref, proj(OFF_V_A, D_CONV) * _from_slabs(conva_ref, slice(0, tm), a_slabs), HALO)

    def step_conv_b():
        b_a = proj(OFF_B_A, D_CONV)
        for s in a_slabs:
            _store_strided(conva_ref, s, _causal_conv_slab(cv_ref, s, caw_ref, CONV_A_WIDTH, tm))
        _keep_halo(cv_ref, tm)
        yain_ref[...] = (b_a * _from_slabs(conva_ref, slice(0, tm), a_slabs)).astype(_BF16)

    def step_ya():
        merged_ref[...] = _dot(yain_ref[...], waout_ref[...])

    def step_gate_a():
        merged_ref[...] = jax.nn.sigmoid(proj(OFF_GATE_A, D_MODEL)) * merged_ref[...]

    def step_gate_s():
        gs_ref[...] = jax.nn.sigmoid(proj(OFF_GATE_S, D_MODEL))

    def step_out(rows):
        y_s = _dot(yn_ref[rows, :], wsout_ref[...])
        merged = merged_ref[rows, :] + gs_ref[rows, :] * y_s
        o_ref[rows, :] = x_ref[rows, :] + _dot(merged.astype(_BF16), wo_ref[...])

    dense_steps = [step_c, step_cv, step_conv_b, step_ya, step_gate_a, step_gate_s]

    def group_prelude(ci, g):
        rows = slice(ci * CHUNK, (ci + 1) * CHUNK)
        gcols = slice(g * GROUP_WIDTH, (g + 1) * GROUP_WIDTH)
        per_chan = _dot(perhead_ref[ci], expand_ref[:, gcols])
        decay_in_full = per_chan[CHUNK:2 * CHUNK]
        xs_g = _from_slabs(xs_ref, rows, range(g * GROUP_SLABS, (g + 1) * GROUP_SLABS))
        x_end = (xs_g * per_chan[0:CHUNK]).astype(_BF16)
        b_g = bc_ref[g, rows, :].astype(_BF16)
        c_g = bc_ref[N_GROUPS + g, rows, :].astype(_BF16)
        cb = lax.dot_general(c_g, b_g, (((1,), (1,)), ((), ())), preferred_element_type=_F32)
        state_prev = state_ref[:, gcols]
        y_off = _dot(c_g, state_prev.astype(_BF16)) * decay_in_full
        if dense_steps:
            dense_steps.pop(0)()
        state_new = lax.dot_general(b_g, x_end, (((0,), (0,)), ((), ())), preferred_element_type=_F32)
        state_ref[:, gcols] = state_prev * decay_in_full[CHUNK - 1:CHUNK, :] + state_new
        return cb, y_off

    def group_body(ci, g, cb, y_off):
        rows = slice(ci * CHUNK, (ci + 1) * CHUNK)
        gcols = slice(g * GROUP_WIDTH, (g + 1) * GROUP_WIDTH)
        y_cols = []
        for jp in range(GROUP_SLABS):
            h0 = g * HEADS_PER_GROUP + 2 * jp
            pcols = slice(h0 * HEAD_DIM, (h0 + 2) * HEAD_DIM)
            lcols = slice(jp * LANES, (jp + 1) * LANES)
            decays = []
            for h in (h0, h0 + 1):
                seg = acum_ref[ci, :, h:h + 1] - acumt_ref[ci, h:h + 1, :]
                decays.append((cb * jnp.exp2(jnp.where(causal, seg, -jnp.inf))).astype(_BF16))
            lhs = jnp.concatenate(decays, axis=1)
            x_f32 = xs_ref[g * GROUP_SLABS + jp, rows, :]
            x_pair = x_f32.astype(_BF16)
            zero = jnp.zeros_like(x_pair)
            rhs = jnp.concatenate([jnp.where(low_half, x_pair, zero),
                                   jnp.where(low_half, zero, x_pair)], axis=0)
            y_cols.append(_dot(lhs, rhs) + y_off[:, lcols] + dskip_ref[:, pcols] * x_f32)
        yz = jnp.concatenate(y_cols, axis=1) * z_ref[rows, gcols]
        yn = yz * lax.rsqrt(jnp.mean(yz * yz, axis=-1, keepdims=True) + EPS) * snw_ref[:, gcols]
        yn_ref[rows, gcols] = yn.astype(_BF16)

    order = [(ci, g) for ci in range(n_chunks) for g in range(N_GROUPS)]
    ahead = group_prelude(*order[0])
    for k, (ci, g) in enumerate(order):
        current = ahead
        if k + 1 < len(order):
            ahead = group_prelude(*order[k + 1])
        group_body(ci, g, *current)
        if g == N_GROUPS - 1:
            dense_steps.append(functools.partial(step_out, slice(ci * CHUNK, (ci + 1) * CHUNK)))
    while dense_steps:
        dense_steps.pop(0)()


def _ffn_kernel(h_ref, nw_ref, wup_hbm, fcw_ref, fcb_ref, wdown_hbm, fnw_ref, o_ref,
                wup_ref, wdown_ref, up_stage_ref, square_stage_ref, stage_sem, h1_ref, act_ref):
    tm = h_ref.shape[0]

    @pl.when(_first_step())
    def _():
        _stream_cast(wup_hbm, wup_ref, up_stage_ref, stage_sem)
        _stream_cast(wdown_hbm, wdown_ref, square_stage_ref, stage_sem)

    @pl.when(pl.program_id(1) == 0)
    def _():
        h1_ref[:, 0:HALO, :] = jnp.zeros((h1_ref.shape[0], HALO, LANES), _F32)

    h = h_ref[...]
    v = _rmsnorm(h, nw_ref[...]).astype(_BF16)
    _to_slabs(h1_ref, _dot(v, wup_ref[:, 0:D_FF]), HALO)
    h3 = _dot(v, wup_ref[:, D_FF:2 * D_FF])
    for s in range(D_FF // LANES):
        cols = slice(s * LANES, (s + 1) * LANES)
        outs = [_silu(o + fcb_ref[:, cols])
                for o in _causal_conv_slab(h1_ref, s, fcw_ref, FFN_CONV_WIDTH, tm)]
        _store_strided(act_ref, s, outs)
    _keep_halo(h1_ref, tm)
    act = _from_slabs(act_ref, slice(0, tm), range(D_FF // LANES))
    out = h + _dot((act * h3).astype(_BF16), wdown_ref[...])
    o_ref[...] = _rmsnorm(out, fnw_ref[...])


def _resident(shape):
    return pl.BlockSpec(shape, lambda b, s: (0,) * len(shape), pipeline_mode=pl.Buffered(1))


def _token_tile(tile):
    return pl.BlockSpec((None, tile, D_MODEL), lambda b, s: (b, s, 0))


_IN_HBM = pl.BlockSpec(memory_space=pl.ANY)


def _mixer_call(x, nw, win, wdt, caw, waout, scw, scb, dtb, alog, dskip, snw, wsout, wo):
    batch, seq, _ = x.shape
    tm = MIXER_TILE
    params = (nw, win, wdt, caw, waout, scw, scb, dtb, alog, dskip, snw, wsout, wo)
    in_hbm = (win, waout, wsout, wo)
    assert win.shape == (N_IN, D_MODEL) and PROJ_BLOCK == SQUARE_STAGE_ROWS and OFF_DT % PROJ_BLOCK == 0
    assert all(w.shape[0] % SQUARE_STAGE_ROWS == 0 and w.shape[1] == D_MODEL for w in (waout, wsout, wo))
    return pl.pallas_call(
        _mixer_kernel,
        out_shape=jax.ShapeDtypeStruct(x.shape, _F32),
        grid=(batch, seq // tm),
        in_specs=[_token_tile(tm)] + [_IN_HBM if any(p is w for w in in_hbm) else _resident(p.shape)
                                      for p in params],
        out_specs=_token_tile(tm),
        scratch_shapes=[
            pltpu.VMEM((OFF_DT // PROJ_BLOCK, D_MODEL, PROJ_BLOCK), _BF16),
            pltpu.VMEM(waout.shape, _BF16),
            pltpu.VMEM(wsout.shape, _BF16),
            pltpu.VMEM(wo.shape, _BF16),
            pltpu.VMEM((HEADS_PAD, D_INNER), _BF16),
            pltpu.VMEM((2, SQUARE_STAGE_ROWS, D_MODEL), _F32),
            pltpu.SemaphoreType.DMA((2,)),
            pltpu.VMEM((tm, D_MODEL), _BF16),
            pltpu.VMEM((D_CONV // LANES, HALO + tm, LANES), _F32),
            pltpu.VMEM((D_CONV // LANES, tm, LANES), _F32),
            pltpu.VMEM((tm, D_CONV), _BF16),
            pltpu.VMEM((D_XBC // LANES, HALO + tm, LANES), _F32),
            pltpu.VMEM((D_INNER // LANES, tm, LANES), _F32),
            pltpu.VMEM((2 * N_GROUPS, tm, D_STATE), _F32),
            pltpu.VMEM((tm, D_INNER), _F32),
            pltpu.VMEM((tm // CHUNK, CHUNK, HEADS_PAD), _F32),
            pltpu.VMEM((tm // CHUNK, HEADS_PAD, CHUNK), _F32),
            pltpu.VMEM((tm // CHUNK, 2 * CHUNK, HEADS_PAD), _BF16),
            pltpu.VMEM((tm, D_INNER), _BF16),
            pltpu.VMEM((tm, D_MODEL), _F32),
            pltpu.VMEM((tm, D_MODEL), _F32),
            pltpu.VMEM((D_STATE, D_INNER), _F32),
        ],
        compiler_params=pltpu.CompilerParams(
            dimension_semantics=("arbitrary", "arbitrary"),
            vmem_limit_bytes=MIXER_VMEM_LIMIT),
        name="mixer",
    )(x, *params)


def _ffn_call(h, nw, wup, fcw, fcb, wdown, fnw):
    batch, seq, _ = h.shape
    tm = FFN_TILE
    params = (nw, wup, fcw, fcb, wdown, fnw)
    in_hbm = (wup, wdown)
    assert wup.shape == (D_MODEL, 2 * D_FF) and D_MODEL % UP_STAGE_ROWS == 0
    assert wdown.shape == (D_FF, D_MODEL) and D_FF % SQUARE_STAGE_ROWS == 0
    return pl.pallas_call(
        _ffn_kernel,
        out_shape=jax.ShapeDtypeStruct(h.shape, _F32),
        grid=(batch, seq // tm),
        in_specs=[_token_tile(tm)] + [_IN_HBM if any(p is w for w in in_hbm) else _resident(p.shape)
                                      for p in params],
        out_specs=_token_tile(tm),
        scratch_shapes=[
            pltpu.VMEM(wup.shape, _BF16),
            pltpu.VMEM(wdown.shape, _BF16),
            pltpu.VMEM((2, UP_STAGE_ROWS, 2 * D_FF), _F32),
            pltpu.VMEM((2, SQUARE_STAGE_ROWS, D_MODEL), _F32),
            pltpu.SemaphoreType.DMA((2,)),
            pltpu.VMEM((D_FF // LANES, HALO + tm, LANES), _F32),
            pltpu.VMEM((D_FF // LANES, tm, LANES), _F32),
        ],
        compiler_params=pltpu.CompilerParams(
            dimension_semantics=("arbitrary", "arbitrary"),
            vmem_limit_bytes=FFN_VMEM_LIMIT),
        name="ffn",
    )(h, *params)


def _row(v, pad_to=None):
    v = v.astype(_F32).reshape(1, -1)
    if pad_to is not None:
        v = jnp.pad(v, ((0, 0), (0, pad_to - v.shape[1])))
    return v


def kernel(x, norm_mix_w, w_in, conv_a_w, w_a_out, ssd_conv_w, ssd_conv_b, dt_bias, a_log, d_skip,
           ssd_norm_w, w_s_out, w_o, norm_ffn_w, w_up, ffn_conv_w, ffn_conv_b, w_down, final_norm_w):
    depth = w_in.shape[0]
    assert x.shape[1] % MIXER_TILE == 0 and x.shape[1] % FFN_TILE == 0
    assert MIXER_TILE % CHUNK == 0
    assert MIXER_TILE % (CONV_STRIDE * SUBLANES) == 0 and FFN_TILE % (CONV_STRIDE * SUBLANES) == 0
    assert HEAD_DIM & (HEAD_DIM - 1) == 0
    h = x
    for l in range(depth):
        wdt = jnp.pad(w_in[l, :, OFF_DT:N_IN], ((0, 0), (0, HEADS_PAD - N_HEADS))).astype(_BF16)
        h = _mixer_call(
            h, _row(norm_mix_w[l]), jnp.swapaxes(w_in[l], 0, 1).astype(_F32), wdt, conv_a_w[l].astype(_F32),
            w_a_out[l].astype(_F32), ssd_conv_w[l].astype(_F32), _row(ssd_conv_b[l]),
            _row(dt_bias[l], HEADS_PAD), _row(a_log[l], HEADS_PAD), _row(jnp.repeat(d_skip[l], HEAD_DIM)),
            _row(ssd_norm_w[l]), w_s_out[l].astype(_F32), w_o[l].astype(_F32))
        assert depth == 1
        h = _ffn_call(h, _row(norm_ffn_w[l]), w_up[l].astype(_F32), ffn_conv_w[l].astype(_F32),
                      _row(ffn_conv_b[l]), w_down[l].astype(_F32), _row(final_norm_w))
    return h
```

```python
import functools

import jax
import jax.numpy as jnp
from jax import lax
from jax.experimental import pallas as pl
from jax.experimental.pallas import tpu as pltpu

D_MODEL = 1024
D_CONV = D_MODEL
CONV_A_WIDTH = 3
D_INNER = 2048
HEAD_DIM = 64
N_HEADS = 32
N_GROUPS = 4
D_STATE = 128
SSD_CONV_WIDTH = 4
CHUNK = 128
D_XBC = D_INNER + 2 * N_GROUPS * D_STATE
D_FF = 2816
FFN_CONV_WIDTH = 3
EPS = 1e-5
LOG2_E = 1.4426950408889634

LANES = 128
SUBLANES = 8
CONV_STRIDE = 4

HEADS_PAD = LANES
OFF_GATE_A = 0
OFF_GATE_S = OFF_GATE_A + D_MODEL
OFF_B_A = OFF_GATE_S + D_MODEL
OFF_C_A = OFF_B_A + D_CONV
OFF_V_A = OFF_C_A + D_CONV
OFF_Z = OFF_V_A + D_CONV
OFF_XBC = OFF_Z + D_INNER
OFF_DT = OFF_XBC + D_XBC
N_IN = OFF_DT + N_HEADS

HEADS_PER_GROUP = N_HEADS // N_GROUPS
GROUP_WIDTH = HEADS_PER_GROUP * HEAD_DIM
GROUP_SLABS = GROUP_WIDTH // LANES
HALO = SUBLANES

MIXER_TILE = 256
FFN_TILE = 512
UP_STAGE_ROWS = 64
SQUARE_STAGE_ROWS = 256
STAGE_SLOTS = 4
PROJ_BLOCK = 256
MIXER_VMEM_LIMIT = 60 * 1024 * 1024
FFN_VMEM_LIMIT = 56 * 1024 * 1024

_F32 = jnp.float32
_BF16 = jnp.bfloat16


def _dot(a, b):
    return jnp.dot(a, b, preferred_element_type=_F32)


def _rmsnorm(x, w):
    return x * lax.rsqrt(jnp.mean(x * x, axis=-1, keepdims=True) + EPS) * w


def _silu(x):
    return x * jax.nn.sigmoid(x)


def _to_slabs(buf_ref, value, row0):
    rows = value.shape[0]
    for s in range(value.shape[1] // LANES):
        buf_ref[s, row0:row0 + rows, :] = value[:, s * LANES:(s + 1) * LANES]


def _from_slabs(buf_ref, rows, slabs):
    return jnp.concatenate([buf_ref[s, rows, :] for s in slabs], axis=1)


def _causal_conv_slab(buf_ref, slab, w_ref, width, rows):
    n = rows // CONV_STRIDE
    first = HALO - (width - 1)
    reads = [buf_ref[slab, pl.ds(j, n, stride=CONV_STRIDE), :]
             for j in range(first, HALO + CONV_STRIDE)]
    cols = slice(slab * LANES, (slab + 1) * LANES)
    taps = [w_ref[k:k + 1, cols] for k in range(width)]
    outs = []
    for r in range(CONV_STRIDE):
        acc = taps[0] * reads[r]
        for k in range(1, width):
            acc = acc + taps[k] * reads[r + k]
        outs.append(acc)
    return outs


def _store_strided(dst_ref, slab, outs):
    n = outs[0].shape[0]
    for r, val in enumerate(outs):
        dst_ref[slab, pl.ds(r, n, stride=CONV_STRIDE), :] = val


def _keep_halo(buf_ref, rows):
    for s in range(buf_ref.shape[0]):
        buf_ref[s, 0:HALO, :] = buf_ref[s, rows:rows + HALO, :]


def _stream_blocks(src_hbm, n_blocks, stage_ref, sem_ref, consume):
    n_slots, block_rows = stage_ref.shape[0], stage_ref.shape[1]
    lookahead = n_slots - 1

    def block_copy(i, slot):
        return pltpu.make_async_copy(src_hbm.at[pl.ds(i * block_rows, block_rows), :],
                                     stage_ref.at[slot], sem_ref.at[slot])

    for i in range(min(lookahead, n_blocks)):
        block_copy(i, i % n_slots).start()

    def body(i, carry):
        @pl.when(i + lookahead < n_blocks)
        def _():
            block_copy(i + lookahead, lax.rem(i + lookahead, n_slots)).start()

        slot = lax.rem(i, n_slots)
        block_copy(i, slot).wait()
        consume(i, stage_ref[slot])
        return carry

    lax.fori_loop(0, n_blocks, body, 0)


def _stream_cast(src_hbm, dst_ref, stage_ref, sem_ref):
    block_rows = stage_ref.shape[1]

    def consume(i, block):
        r0 = pl.multiple_of(i * block_rows, block_rows)
        dst_ref[pl.ds(r0, block_rows), :] = block.astype(_BF16)

    _stream_blocks(src_hbm, dst_ref.shape[0] // block_rows, stage_ref, sem_ref, consume)


def _stream_cast_transposed(src_hbm, dst_ref, stage_ref, sem_ref):
    def consume(i, block):
        dst_ref[i] = block.T.astype(_BF16)

    _stream_blocks(src_hbm, dst_ref.shape[0], stage_ref, sem_ref, consume)


def _first_step():
    return jnp.logical_and(pl.program_id(0) == 0, pl.program_id(1) == 0)


def _split3(x):
    hi = x.astype(_BF16)
    r1 = x - hi.astype(_F32)
    mid = r1.astype(_BF16)
    lo = (r1 - mid.astype(_F32)).astype(_BF16)
    return hi, mid, lo


def _mixer_kernel(x_ref, nw_ref, win_hbm, wdt_ref, caw_ref, waout_hbm, scw_ref, scb_ref, dtb_ref, alog_ref,
                  dskip_ref, snw_ref, wsout_hbm, wo_hbm,
                  o_ref,
                  win_ref, waout_ref, wsout_ref, wo_ref, expand_ref, square_stage_ref, stage_sem,
                  u_ref, cv_ref, conva_ref, yain_ref, xbc_ref, xs_ref, bc_ref, z_ref, acum_ref,
                  acumt_ref, perhead_ref, yn_ref, merged_ref, gs_ref, state_ref):
    tm = x_ref.shape[0]
    n_chunks = tm // CHUNK
    n_xs_slabs = D_INNER // LANES

    @pl.when(_first_step())
    def _():
        _stream_cast_transposed(win_hbm, win_ref, square_stage_ref, stage_sem)
        _stream_cast(waout_hbm, waout_ref, square_stage_ref, stage_sem)
        _stream_cast(wsout_hbm, wsout_ref, square_stage_ref, stage_sem)
        _stream_cast(wo_hbm, wo_ref, square_stage_ref, stage_sem)
        head = lax.broadcasted_iota(jnp.int32, (HEADS_PAD, D_INNER), 0)
        chan = lax.broadcasted_iota(jnp.int32, (HEADS_PAD, D_INNER), 1)
        owner = lax.shift_right_logical(chan, HEAD_DIM.bit_length() - 1)
        expand_ref[...] = jnp.where(head == owner, 1.0, 0.0).astype(_BF16)

    @pl.when(pl.program_id(1) == 0)
    def _():
        cv_ref[:, 0:HALO, :] = jnp.zeros((cv_ref.shape[0], HALO, LANES), _F32)
        xbc_ref[:, 0:HALO, :] = jnp.zeros((xbc_ref.shape[0], HALO, LANES), _F32)
        state_ref[...] = jnp.zeros(state_ref.shape, _F32)

    x = x_ref[...]
    u_ref[...] = _rmsnorm(x, nw_ref[...]).astype(_BF16)

    def proj(off, width):
        blocks = range(off // PROJ_BLOCK, (off + width) // PROJ_BLOCK)
        return jnp.concatenate([_dot(u_ref[...], win_ref[j]) for j in blocks], axis=1)

    a_row = -jnp.exp(alog_ref[...])
    row_id = lax.broadcasted_iota(jnp.int32, (CHUNK, CHUNK), 0)
    col_id = lax.broadcasted_iota(jnp.int32, (CHUNK, CHUNK), 1)
    causal = row_id >= col_id
    tril = causal.astype(_BF16)
    low_half = lax.broadcasted_iota(jnp.int32, (CHUNK, LANES), 1) < HEAD_DIM

    _to_slabs(xbc_ref, proj(OFF_XBC, D_XBC), HALO)

    dt_raw = _dot(u_ref[...], wdt_ref[...]) + dtb_ref[...]
    z_ref[...] = _silu(proj(OFF_Z, D_INNER))
    dt_all =jnp.maximum(dt_raw, 0.0) + jnp.log1p(jnp.exp(-jnp.abs(dt_raw)))
    for ci in range(n_chunks):
        dt = dt_all[ci * CHUNK:(ci + 1) * CHUNK, :]
        hi, mid, lo = _split3(dt * a_row)
        a_cum = _dot(tril, hi) + _dot(tril, mid) + _dot(tril, lo)
        a2 = a_cum * LOG2_E
        acum_ref[ci] = a2
        acumt_ref[ci] = (a2 - jnp.log(dt) * LOG2_E).T
        decay_to_end = jnp.exp2(a2[CHUNK - 1:CHUNK, :] - a2)
        decay_in = jnp.exp2(a2)
        perhead_ref[ci] = jnp.concatenate([dt * decay_to_end, decay_in], axis=0).astype(_BF16)

    for s in range(D_XBC // LANES):
        cols = slice(s * LANES, (s + 1) * LANES)
        outs = [_silu(o + scb_ref[:, cols])
                for o in _causal_conv_slab(xbc_ref, s, scw_ref, SSD_CONV_WIDTH, tm)]
        if s < n_xs_slabs:
            _store_strided(xs_ref, s, outs)
        else:
            _store_strided(bc_ref, s - n_xs_slabs, outs)
    _keep_halo(xbc_ref, tm)

    a_slabs = range(D_CONV // LANES)

    def step_c():
        _to_slabs(conva_ref, proj(OFF_C_A, D_CONV), 0)

    def step_cv():
        _to_slabs(cv_ref, proj(OFF_V_A, D_CONV) * _from_slabs(conva_ref, slice(0, tm), a_slabs), HALO)

    def step_conv_b():
        b_a = proj(OFF_B_A, D_CONV)
        for s in a_slabs:
            _store_strided(conva_ref, s, _causal_conv_slab(cv_ref, s, caw_ref, CONV_A_WIDTH, tm))
        _keep_halo(cv_ref, tm)
        yain_ref[...] = (b_a * _from_slabs(conva_ref, slice(0, tm), a_slabs)).astype(_BF16)

    def step_ya():
        merged_ref[...] = _dot(yain_ref[...], waout_ref[...])

    def step_gate_a():
        merged_ref[...] = jax.nn.sigmoid(proj(OFF_GATE_A, D_MODEL)) * merged_ref[...]

    def step_gate_s():
        gs_ref[...] = jax.nn.sigmoid(proj(OFF_GATE_S, D_MODEL))

    def step_out(rows):
        y_s = _dot(yn_ref[rows, :], wsout_ref[...])
        merged = merged_ref[rows, :] + gs_ref[rows, :] * y_s
        o_ref[rows, :] = x_ref[rows, :] + _dot(merged.astype(_BF16), wo_ref[...])

    dense_steps = [step_c, step_cv, step_conv_b, step_ya, step_gate_a, step_gate_s]

    def group_prelude(ci, g):
        rows = slice(ci * CHUNK, (ci + 1) * CHUNK)
        gcols = slice(g * GROUP_WIDTH, (g + 1) * GROUP_WIDTH)
        per_chan = _dot(perhead_ref[ci], expand_ref[:, gcols])
        decay_in_full = per_chan[CHUNK:2 * CHUNK]
        xs_g = _from_slabs(xs_ref, rows, range(g * GROUP_SLABS, (g + 1) * GROUP_SLABS))
        x_end = (xs_g * per_chan[0:CHUNK]).astype(_BF16)
        b_g = bc_ref[g, rows, :].astype(_BF16)
        c_g = bc_ref[N_GROUPS + g, rows, :].astype(_BF16)
        cb = lax.dot_general(c_g, b_g, (((1,), (1,)), ((), ())), preferred_element_type=_F32)
        state_prev = state_ref[:, gcols]
        y_off = _dot(c_g, state_prev.astype(_BF16)) * decay_in_full
        if dense_steps:
            dense_steps.pop(0)()
        state_new = lax.dot_general(b_g, x_end, (((0,), (0,)), ((), ())), preferred_element_type=_F32)
        state_ref[:, gcols] = state_prev * decay_in_full[CHUNK - 1:CHUNK, :] + state_new
        return cb, y_off

    def group_body(ci, g, cb, y_off):
        rows = slice(ci * CHUNK, (ci + 1) * CHUNK)
        gcols = slice(g * GROUP_WIDTH, (g + 1) * GROUP_WIDTH)
        y_cols = []
        for jp in range(GROUP_SLABS):
            h0 = g * HEADS_PER_GROUP + 2 * jp
            pcols = slice(h0 * HEAD_DIM, (h0 + 2) * HEAD_DIM)
            lcols = slice(jp * LANES, (jp + 1) * LANES)
            decays = []
            for h in (h0, h0 + 1):
                seg = acum_ref[ci, :, h:h + 1] - acumt_ref[ci, h:h + 1, :]
                decays.append((cb * jnp.exp2(jnp.where(causal, seg, -jnp.inf))).astype(_BF16))
            lhs = jnp.concatenate(decays, axis=1)
            x_f32 = xs_ref[g * GROUP_SLABS + jp, rows, :]
            x_pair = x_f32.astype(_BF16)
            zero = jnp.zeros_like(x_pair)
            rhs = jnp.concatenate([jnp.where(low_half, x_pair, zero),
                                   jnp.where(low_half, zero, x_pair)], axis=0)
            y_cols.append(_dot(lhs, rhs) + y_off[:, lcols] + dskip_ref[:, pcols] * x_f32)
        yz = jnp.concatenate(y_cols, axis=1) * z_ref[rows, gcols]
        yn = yz * lax.rsqrt(jnp.mean(yz * yz, axis=-1, keepdims=True) + EPS) * snw_ref[:, gcols]
        yn_ref[rows, gcols] = yn.astype(_BF16)

    order = [(ci, g) for ci in range(n_chunks) for g in range(N_GROUPS)]
    ahead = group_prelude(*order[0])
    for k, (ci, g) in enumerate(order):
        current = ahead
        if k + 1 < len(order):
            ahead = group_prelude(*order[k + 1])
        group_body(ci, g, *current)
        if g == N_GROUPS - 1:
            dense_steps.append(functools.partial(step_out, slice(ci * CHUNK, (ci + 1) * CHUNK)))
    while dense_steps:
        dense_steps.pop(0)()


def _ffn_kernel(h_ref, nw_ref, wup_hbm, fcw_ref, fcb_ref, wdown_hbm, fnw_ref, o_ref,
                wup_ref, wdown_ref, up_stage_ref, square_stage_ref, stage_sem, h1_ref, act_ref):
    tm = h_ref.shape[0]

    @pl.when(_first_step())
    def _():
        _stream_cast(wup_hbm, wup_ref, up_stage_ref, stage_sem)
        _stream_cast(wdown_hbm, wdown_ref, square_stage_ref, stage_sem)

    @pl.when(pl.program_id(1) == 0)
    def _():
        h1_ref[:, 0:HALO, :] = jnp.zeros((h1_ref.shape[0], HALO, LANES), _F32)

    h = h_ref[...]
    v = _rmsnorm(h, nw_ref[...]).astype(_BF16)
    _to_slabs(h1_ref, _dot(v, wup_ref[:, 0:D_FF]), HALO)
    h3 = _dot(v, wup_ref[:, D_FF:2 * D_FF])
    for s in range(D_FF // LANES):
        cols = slice(s * LANES, (s + 1) * LANES)
        outs = [_silu(o + fcb_ref[:, cols])
                for o in _causal_conv_slab(h1_ref, s, fcw_ref, FFN_CONV_WIDTH, tm)]
        _store_strided(act_ref, s, outs)
    _keep_halo(h1_ref, tm)
    act = _from_slabs(act_ref, slice(0, tm), range(D_FF // LANES))
    out = h + _dot((act * h3).astype(_BF16), wdown_ref[...])
    o_ref[...] = _rmsnorm(out, fnw_ref[...])


def _resident(shape):
    return pl.BlockSpec(shape, lambda b, s: (0,) * len(shape), pipeline_mode=pl.Buffered(1))


def _token_tile(tile):
    return pl.BlockSpec((None, tile, D_MODEL), lambda b, s: (b, s, 0))


_IN_HBM = pl.BlockSpec(memory_space=pl.ANY)


def _mixer_call(x, nw, win, wdt, caw, waout, scw, scb, dtb, alog, dskip, snw, wsout, wo):
    batch, seq, _ = x.shape
    tm = MIXER_TILE
    params = (nw, win, wdt, caw, waout, scw, scb, dtb, alog, dskip, snw, wsout, wo)
    in_hbm = (win, waout, wsout, wo)
    assert win.shape == (N_IN, D_MODEL) and PROJ_BLOCK == SQUARE_STAGE_ROWS and OFF_DT % PROJ_BLOCK == 0
    assert all(w.shape[0] % SQUARE_STAGE_ROWS == 0 and w.shape[1] == D_MODEL for w in (waout, wsout, wo))
    return pl.pallas_call(
        _mixer_kernel,
        out_shape=jax.ShapeDtypeStruct(x.shape, _F32),
        grid=(batch, seq // tm),
        in_specs=[_token_tile(tm)] + [_IN_HBM if any(p is w for w in in_hbm) else _resident(p.shape)
                                      for p in params],
        out_specs=_token_tile(tm),
        scratch_shapes=[
            pltpu.VMEM((OFF_DT // PROJ_BLOCK, D_MODEL, PROJ_BLOCK), _BF16),
            pltpu.VMEM(waout.shape, _BF16),
            pltpu.VMEM(wsout.shape, _BF16),
            pltpu.VMEM(wo.shape, _BF16),
            pltpu.VMEM((HEADS_PAD, D_INNER), _BF16),
            pltpu.VMEM((STAGE_SLOTS, SQUARE_STAGE_ROWS, D_MODEL), _F32),
            pltpu.SemaphoreType.DMA((STAGE_SLOTS,)),
            pltpu.VMEM((tm, D_MODEL), _BF16),
            pltpu.VMEM((D_CONV // LANES, HALO + tm, LANES), _F32),
            pltpu.VMEM((D_CONV // LANES, tm, LANES), _F32),
            pltpu.VMEM((tm, D_CONV), _BF16),
            pltpu.VMEM((D_XBC // LANES, HALO + tm, LANES), _F32),
            pltpu.VMEM((D_INNER // LANES, tm, LANES), _F32),
            pltpu.VMEM((2 * N_GROUPS, tm, D_STATE), _F32),
            pltpu.VMEM((tm, D_INNER), _F32),
            pltpu.VMEM((tm // CHUNK, CHUNK, HEADS_PAD), _F32),
            pltpu.VMEM((tm // CHUNK, HEADS_PAD, CHUNK), _F32),
            pltpu.VMEM((tm // CHUNK, 2 * CHUNK, HEADS_PAD), _BF16),
            pltpu.VMEM((tm, D_INNER), _BF16),
            pltpu.VMEM((tm, D_MODEL), _F32),
            pltpu.VMEM((tm, D_MODEL), _F32),
            pltpu.VMEM((D_STATE, D_INNER), _F32),
        ],
        compiler_params=pltpu.CompilerParams(
            dimension_semantics=("arbitrary", "arbitrary"),
            vmem_limit_bytes=MIXER_VMEM_LIMIT),
        name="mixer",
    )(x, *params)


def _ffn_call(h, nw, wup, fcw, fcb, wdown, fnw):
    batch, seq, _ = h.shape
    tm = FFN_TILE
    params = (nw, wup, fcw, fcb, wdown, fnw)
    in_hbm = (wup, wdown)
    assert wup.shape == (D_MODEL, 2 * D_FF) and D_MODEL % UP_STAGE_ROWS == 0
    assert wdown.shape == (D_FF, D_MODEL) and D_FF % SQUARE_STAGE_ROWS == 0
    return pl.pallas_call(
        _ffn_kernel,
        out_shape=jax.ShapeDtypeStruct(h.shape, _F32),
        grid=(batch, seq // tm),
        in_specs=[_token_tile(tm)] + [_IN_HBM if any(p is w for w in in_hbm) else _resident(p.shape)
                                      for p in params],
        out_specs=_token_tile(tm),
        scratch_shapes=[
            pltpu.VMEM(wup.shape, _BF16),
            pltpu.VMEM(wdown.shape, _BF16),
            pltpu.VMEM((STAGE_SLOTS, UP_STAGE_ROWS, 2 * D_FF), _F32),
            pltpu.VMEM((STAGE_SLOTS, SQUARE_STAGE_ROWS, D_MODEL), _F32),
            pltpu.SemaphoreType.DMA((STAGE_SLOTS,)),
            pltpu.VMEM((D_FF // LANES, HALO + tm, LANES), _F32),
            pltpu.VMEM((D_FF // LANES, tm, LANES), _F32),
        ],
        compiler_params=pltpu.CompilerParams(
            dimension_semantics=("arbitrary", "arbitrary"),
            vmem_limit_bytes=FFN_VMEM_LIMIT),
        name="ffn",
    )(h, *params)


def _row(v, pad_to=None):
    v = v.astype(_F32).reshape(1, -1)
    if pad_to is not None:
        v = jnp.pad(v, ((0, 0), (0, pad_to - v.shape[1])))
    return v


def kernel(x, norm_mix_w, w_in, conv_a_w, w_a_out, ssd_conv_w, ssd_conv_b, dt_bias, a_log, d_skip,
           ssd_norm_w, w_s_out, w_o, norm_ffn_w, w_up, ffn_conv_w, ffn_conv_b, w_down, final_norm_w):
    depth = w_in.shape[0]
    assert x.shape[1] % MIXER_TILE == 0 and x.shape[1] % FFN_TILE == 0
    assert MIXER_TILE % CHUNK == 0
    assert MIXER_TILE % (CONV_STRIDE * SUBLANES) == 0 and FFN_TILE % (CONV_STRIDE * SUBLANES) == 0
    assert HEAD_DIM & (HEAD_DIM - 1) == 0
    h = x
    for l in range(depth):
        wdt = jnp.pad(w_in[l, :, OFF_DT:N_IN], ((0, 0), (0, HEADS_PAD - N_HEADS))).astype(_BF16)
        h = _mixer_call(
            h, _row(norm_mix_w[l]), jnp.swapaxes(w_in[l], 0, 1).astype(_F32), wdt, conv_a_w[l].astype(_F32),
            w_a_out[l].astype(_F32), ssd_conv_w[l].astype(_F32), _row(ssd_conv_b[l]),
            _row(dt_bias[l], HEADS_PAD), _row(a_log[l], HEADS_PAD), _row(jnp.repeat(d_skip[l], HEAD_DIM)),
            _row(ssd_norm_w[l]), w_s_out[l].astype(_F32), w_o[l].astype(_F32))
        assert depth == 1
        h = _ffn_call(h, _row(norm_ffn_w[l]), w_up[l].astype(_F32), ffn_conv_w[l].astype(_F32),
                      _row(ffn_conv_b[l]), w_down[l].astype(_F32), _row(final_norm_w))
    return h
```

```python
import functools

import jax
import jax.numpy as jnp
from jax import lax
from jax.experimental import pallas as pl
from jax.experimental.pallas import tpu as pltpu

D_MODEL = 1024
D_CONV = D_MODEL
CONV_A_WIDTH = 3
D_INNER = 2048
HEAD_DIM = 64
N_HEADS = 32
N_GROUPS = 4
D_STATE = 128
SSD_CONV_WIDTH = 4
CHUNK = 128
D_XBC = D_INNER + 2 * N_GROUPS * D_STATE
D_FF = 2816
FFN_CONV_WIDTH = 3
EPS = 1e-5
LOG2_E = 1.4426950408889634

LANES = 128
SUBLANES = 8
CONV_STRIDE = 4

HEADS_PAD = LANES
OFF_GATE_A = 0
OFF_GATE_S = OFF_GATE_A + D_MODEL
OFF_B_A = OFF_GATE_S + D_MODEL
OFF_C_A = OFF_B_A + D_CONV
OFF_V_A = OFF_C_A + D_CONV
OFF_Z = OFF_V_A + D_CONV
OFF_XBC = OFF_Z + D_INNER
OFF_DT = OFF_XBC + D_XBC
N_IN = OFF_DT + N_HEADS

HEADS_PER_GROUP = N_HEADS // N_GROUPS
GROUP_WIDTH = HEADS_PER_GROUP * HEAD_DIM
GROUP_SLABS = GROUP_WIDTH // LANES
HALO = SUBLANES

MIXER_TILE = 256
FFN_TILE = 512
UP_STAGE_ROWS = 64
SQUARE_STAGE_ROWS = 256
STAGE_SLOTS = 4
PROJ_BLOCK = 256
MIXER_VMEM_LIMIT = 60 * 1024 * 1024
FFN_VMEM_LIMIT = 56 * 1024 * 1024

_F32 = jnp.float32
_BF16 = jnp.bfloat16


def _dot(a, b):
    return jnp.dot(a, b, preferred_element_type=_F32)


def _rmsnorm(x, w):
    return x * lax.rsqrt(jnp.mean(x * x, axis=-1, keepdims=True) + EPS) * w


def _silu(x):
    return x * jax.nn.sigmoid(x)


def _to_slabs(buf_ref, value, row0):
    rows = value.shape[0]
    for s in range(value.shape[1] // LANES):
        buf_ref[s, row0:row0 + rows, :] = value[:, s * LANES:(s + 1) * LANES]


def _from_slabs(buf_ref, rows, slabs):
    return jnp.concatenate([buf_ref[s, rows, :] for s in slabs], axis=1)


def _causal_conv_slab(buf_ref, slab, w_ref, width, rows):
    n = rows // CONV_STRIDE
    first = HALO - (width - 1)
    reads = [buf_ref[slab, pl.ds(j, n, stride=CONV_STRIDE), :]
             for j in range(first, HALO + CONV_STRIDE)]
    cols = slice(slab * LANES, (slab + 1) * LANES)
    taps = [w_ref[k:k + 1, cols] for k in range(width)]
    outs = []
    for r in range(CONV_STRIDE):
        acc = taps[0] * reads[r]
        for k in range(1, width):
            acc = acc + taps[k] * reads[r + k]
        outs.append(acc)
    return outs


def _store_strided(dst_ref, slab, outs):
    n = outs[0].shape[0]
    for r, val in enumerate(outs):
        dst_ref[slab, pl.ds(r, n, stride=CONV_STRIDE), :] = val


def _keep_halo(buf_ref, rows):
    for s in range(buf_ref.shape[0]):
        buf_ref[s, 0:HALO, :] = buf_ref[s, rows:rows + HALO, :]


def _stream_blocks(src_hbm, n_blocks, stage_ref, sem_ref, consume):
    n_slots, block_rows = stage_ref.shape[0], stage_ref.shape[1]
    lookahead = n_slots - 1

    def block_copy(i, slot):
        return pltpu.make_async_copy(src_hbm.at[pl.ds(i * block_rows, block_rows), :],
                                     stage_ref.at[slot], sem_ref.at[slot])

    for i in range(min(lookahead, n_blocks)):
        block_copy(i, i % n_slots).start()

    def body(i, carry):
        @pl.when(i + lookahead < n_blocks)
        def _():
            block_copy(i + lookahead, lax.rem(i + lookahead, n_slots)).start()

        slot = lax.rem(i, n_slots)
        block_copy(i, slot).wait()
        consume(i, stage_ref[slot])
        return carry

    lax.fori_loop(0, n_blocks, body, 0)


def _stream_cast(src_hbm, dst_ref, stage_ref, sem_ref):
    block_rows = stage_ref.shape[1]

    def consume(i, block):
        r0 = pl.multiple_of(i * block_rows, block_rows)
        dst_ref[pl.ds(r0, block_rows), :] = block.astype(_BF16)

    _stream_blocks(src_hbm, dst_ref.shape[0] // block_rows, stage_ref, sem_ref, consume)


def _stream_cast_transposed(src_hbm, dst_ref, stage_ref, sem_ref):
    def consume(i, block):
        dst_ref[i] = block.T.astype(_BF16)

    _stream_blocks(src_hbm, dst_ref.shape[0], stage_ref, sem_ref, consume)


def _first_step():
    return jnp.logical_and(pl.program_id(0) == 0, pl.program_id(1) == 0)


def _split3(x):
    hi = x.astype(_BF16)
    r1 = x - hi.astype(_F32)
    mid = r1.astype(_BF16)
    lo = (r1 - mid.astype(_F32)).astype(_BF16)
    return hi, mid, lo


def _mixer_kernel(x_ref, nw_ref, win_hbm, wdt_ref, caw_ref, waout_hbm, scw_ref, scb_ref, dtb_ref, alog_ref,
                  dskip_ref, snw_ref, wsout_hbm, wo_hbm,
                  o_ref,
                  win_ref, waout_ref, wsout_ref, wo_ref, square_stage_ref, stage_sem,
                  u_ref, cv_ref, conva_ref, yain_ref, xbc_ref, xs_ref, bc_ref, z_ref, acum_ref,
                  acumt_ref, dtend_ref, decin_ref, yn_ref, merged_ref, gs_ref, state_ref):
    tm = x_ref.shape[0]
    n_chunks = tm // CHUNK
    n_xs_slabs = D_INNER // LANES

    @pl.when(_first_step())
    def _():
        _stream_cast_transposed(win_hbm, win_ref, square_stage_ref, stage_sem)
        _stream_cast(waout_hbm, waout_ref, square_stage_ref, stage_sem)
        _stream_cast(wsout_hbm, wsout_ref, square_stage_ref, stage_sem)
        _stream_cast(wo_hbm, wo_ref, square_stage_ref, stage_sem)

    @pl.when(pl.program_id(1) == 0)
    def _():
        cv_ref[:, 0:HALO, :] = jnp.zeros((cv_ref.shape[0], HALO, LANES), _F32)
        xbc_ref[:, 0:HALO, :] = jnp.zeros((xbc_ref.shape[0], HALO, LANES), _F32)
        state_ref[...] = jnp.zeros(state_ref.shape, _F32)

    x = x_ref[...]
    u_ref[...] = _rmsnorm(x, nw_ref[...]).astype(_BF16)

    def proj(off, width):
        blocks = range(off // PROJ_BLOCK, (off + width) // PROJ_BLOCK)
        return jnp.concatenate([_dot(u_ref[...], win_ref[j]) for j in blocks], axis=1)

    a_row = -jnp.exp(alog_ref[...])
    row_id = lax.broadcasted_iota(jnp.int32, (CHUNK, CHUNK), 0)
    col_id = lax.broadcasted_iota(jnp.int32, (CHUNK, CHUNK), 1)
    causal = row_id >= col_id
    tril = causal.astype(_BF16)
    low_half = lax.broadcasted_iota(jnp.int32, (CHUNK, LANES), 1) < HEAD_DIM

    _to_slabs(xbc_ref, proj(OFF_XBC, D_XBC), HALO)

    dt_raw = _dot(u_ref[...], wdt_ref[...]) + dtb_ref[...]
    z_ref[...] = _silu(proj(OFF_Z, D_INNER))
    dt_all =jnp.maximum(dt_raw, 0.0) + jnp.log1p(jnp.exp(-jnp.abs(dt_raw)))
    for ci in range(n_chunks):
        dt = dt_all[ci * CHUNK:(ci + 1) * CHUNK, :]
        hi, mid, lo = _split3(dt * a_row)
        a_cum = _dot(tril, hi) + _dot(tril, mid) + _dot(tril, lo)
        a2 = a_cum * LOG2_E
        acum_ref[ci] = a2
        acumt_ref[ci] = (a2 - jnp.log(dt) * LOG2_E).T
        decay_to_end = jnp.exp2(a2[CHUNK - 1:CHUNK, :] - a2)
        decay_in = jnp.exp2(a2)
        dtend_ref[ci] = dt * decay_to_end
        decin_ref[ci] = decay_in

    for s in range(D_XBC // LANES):
        cols = slice(s * LANES, (s + 1) * LANES)
        outs = [_silu(o + scb_ref[:, cols])
                for o in _causal_conv_slab(xbc_ref, s, scw_ref, SSD_CONV_WIDTH, tm)]
        if s < n_xs_slabs:
            _store_strided(xs_ref, s, outs)
        else:
            _store_strided(bc_ref, s - n_xs_slabs, outs)
    _keep_halo(xbc_ref, tm)

    a_slabs = range(D_CONV // LANES)

    def step_c():
        _to_slabs(conva_ref, proj(OFF_C_A, D_CONV), 0)

    def step_cv():
        _to_slabs(cv_ref, proj(OFF_V_A, D_CONV) * _from_slabs(conva_ref, slice(0, tm), a_slabs), HALO)

    def step_conv_b():
        b_a = proj(OFF_B_A, D_CONV)
        for s in a_slabs:
            _store_strided(conva_ref, s, _causal_conv_slab(cv_ref, s, caw_ref, CONV_A_WIDTH, tm))
        _keep_halo(cv_ref, tm)
        yain_ref[...] = (b_a * _from_slabs(conva_ref, slice(0, tm), a_slabs)).astype(_BF16)

    def step_ya():
        merged_ref[...] = _dot(yain_ref[...], waout_ref[...])

    def step_gate_a():
        merged_ref[...] = jax.nn.sigmoid(proj(OFF_GATE_A, D_MODEL)) * merged_ref[...]

    def step_gate_s():
        gs_ref[...] = jax.nn.sigmoid(proj(OFF_GATE_S, D_MODEL))

    def step_out(rows):
        y_s = _dot(yn_ref[rows, :], wsout_ref[...])
        merged = merged_ref[rows, :] + gs_ref[rows, :] * y_s
        o_ref[rows, :] = x_ref[rows, :] + _dot(merged.astype(_BF16), wo_ref[...])

    dense_steps = [step_c, step_cv, step_conv_b, step_ya, step_gate_a, step_gate_s]

    def group_prelude(ci, g):
        rows = slice(ci * CHUNK, (ci + 1) * CHUNK)
        gcols = slice(g * GROUP_WIDTH, (g + 1) * GROUP_WIDTH)
        x_end_cols, decay_in_cols = [], []
        for jp in range(GROUP_SLABS):
            h0 = g * HEADS_PER_GROUP + 2 * jp
            to_end = jnp.where(low_half, dtend_ref[ci, :, h0:h0 + 1], dtend_ref[ci, :, h0 + 1:h0 + 2])
            x_end_cols.append((xs_ref[g * GROUP_SLABS + jp, rows, :] * to_end).astype(_BF16))
            decay_in_cols.append(jnp.where(low_half, decin_ref[ci, :, h0:h0 + 1],
                                           decin_ref[ci, :, h0 + 1:h0 + 2]))
        x_end = jnp.concatenate(x_end_cols, axis=1)
        decay_in_full = jnp.concatenate(decay_in_cols, axis=1)
        b_g = bc_ref[g, rows, :].astype(_BF16)
        c_g = bc_ref[N_GROUPS + g, rows, :].astype(_BF16)
        cb = lax.dot_general(c_g, b_g, (((1,), (1,)), ((), ())), preferred_element_type=_F32)
        state_prev = state_ref[:, gcols]
        y_off = _dot(c_g, state_prev.astype(_BF16)) * decay_in_full
        if dense_steps:
            dense_steps.pop(0)()
        state_new = lax.dot_general(b_g, x_end, (((0,), (0,)), ((), ())), preferred_element_type=_F32)
        state_ref[:, gcols] = state_prev * decay_in_full[CHUNK - 1:CHUNK, :] + state_new
        return cb, y_off

    def group_body(ci, g, cb, y_off):
        rows = slice(ci * CHUNK, (ci + 1) * CHUNK)
        gcols = slice(g * GROUP_WIDTH, (g + 1) * GROUP_WIDTH)
        y_cols = []
        for jp in range(GROUP_SLABS):
            h0 = g * HEADS_PER_GROUP + 2 * jp
            pcols = slice(h0 * HEAD_DIM, (h0 + 2) * HEAD_DIM)
            lcols = slice(jp * LANES, (jp + 1) * LANES)
            decays = []
            for h in (h0, h0 + 1):
                seg = acum_ref[ci, :, h:h + 1] - acumt_ref[ci, h:h + 1, :]
                decays.append((cb * jnp.exp2(jnp.where(causal, seg, -jnp.inf))).astype(_BF16))
            lhs = jnp.concatenate(decays, axis=1)
            x_f32 = xs_ref[g * GROUP_SLABS + jp, rows, :]
            x_pair = x_f32.astype(_BF16)
            zero = jnp.zeros_like(x_pair)
            rhs = jnp.concatenate([jnp.where(low_half, x_pair, zero),
                                   jnp.where(low_half, zero, x_pair)], axis=0)
            y_cols.append(_dot(lhs, rhs) + y_off[:, lcols] + dskip_ref[:, pcols] * x_f32)
        yz = jnp.concatenate(y_cols, axis=1) * z_ref[rows, gcols]
        yn = yz * lax.rsqrt(jnp.mean(yz * yz, axis=-1, keepdims=True) + EPS) * snw_ref[:, gcols]
        yn_ref[rows, gcols] = yn.astype(_BF16)

    order = [(ci, g) for ci in range(n_chunks) for g in range(N_GROUPS)]
    ahead = group_prelude(*order[0])
    for k, (ci, g) in enumerate(order):
        current = ahead
        if k + 1 < len(order):
            ahead = group_prelude(*order[k + 1])
        group_body(ci, g, *current)
        if g == N_GROUPS - 1:
            dense_steps.append(functools.partial(step_out, slice(ci * CHUNK, (ci + 1) * CHUNK)))
    while dense_steps:
        dense_steps.pop(0)()


def _ffn_kernel(h_ref, nw_ref, wup_hbm, fcw_ref, fcb_ref, wdown_hbm, fnw_ref, o_ref,
                wup_ref, wdown_ref, up_stage_ref, square_stage_ref, stage_sem, h1_ref, act_ref):
    tm = h_ref.shape[0]

    @pl.when(_first_step())
    def _():
        _stream_cast(wup_hbm, wup_ref, up_stage_ref, stage_sem)
        _stream_cast(wdown_hbm, wdown_ref, square_stage_ref, stage_sem)

    @pl.when(pl.program_id(1) == 0)
    def _():
        h1_ref[:, 0:HALO, :] = jnp.zeros((h1_ref.shape[0], HALO, LANES), _F32)

    h = h_ref[...]
    v = _rmsnorm(h, nw_ref[...]).astype(_BF16)
    _to_slabs(h1_ref, _dot(v, wup_ref[:, 0:D_FF]), HALO)
    h3 = _dot(v, wup_ref[:, D_FF:2 * D_FF])
    for s in range(D_FF // LANES):
        cols = slice(s * LANES, (s + 1) * LANES)
        outs = [_silu(o + fcb_ref[:, cols])
                for o in _causal_conv_slab(h1_ref, s, fcw_ref, FFN_CONV_WIDTH, tm)]
        _store_strided(act_ref, s, outs)
    _keep_halo(h1_ref, tm)
    act = _from_slabs(act_ref, slice(0, tm), range(D_FF // LANES))
    out = h + _dot((act * h3).astype(_BF16), wdown_ref[...])
    o_ref[...] = _rmsnorm(out, fnw_ref[...])


def _resident(shape):
    return pl.BlockSpec(shape, lambda b, s: (0,) * len(shape), pipeline_mode=pl.Buffered(1))


def _token_tile(tile):
    return pl.BlockSpec((None, tile, D_MODEL), lambda b, s: (b, s, 0))


_IN_HBM = pl.BlockSpec(memory_space=pl.ANY)


def _mixer_call(x, nw, win, wdt, caw, waout, scw, scb, dtb, alog, dskip, snw, wsout, wo):
    batch, seq, _ = x.shape
    tm = MIXER_TILE
    params = (nw, win, wdt, caw, waout, scw, scb, dtb, alog, dskip, snw, wsout, wo)
    in_hbm = (win, waout, wsout, wo)
    assert win.shape == (N_IN, D_MODEL) and PROJ_BLOCK == SQUARE_STAGE_ROWS and OFF_DT % PROJ_BLOCK == 0
    assert all(w.shape[0] % SQUARE_STAGE_ROWS == 0 and w.shape[1] == D_MODEL for w in (waout, wsout, wo))
    return pl.pallas_call(
        _mixer_kernel,
        out_shape=jax.ShapeDtypeStruct(x.shape, _F32),
        grid=(batch, seq // tm),
        in_specs=[_token_tile(tm)] + [_IN_HBM if any(p is w for w in in_hbm) else _resident(p.shape)
                                      for p in params],
        out_specs=_token_tile(tm),
        scratch_shapes=[
            pltpu.VMEM((OFF_DT // PROJ_BLOCK, D_MODEL, PROJ_BLOCK), _BF16),
            pltpu.VMEM(waout.shape, _BF16),
            pltpu.VMEM(wsout.shape, _BF16),
            pltpu.VMEM(wo.shape, _BF16),
            pltpu.VMEM((STAGE_SLOTS, SQUARE_STAGE_ROWS, D_MODEL), _F32),
            pltpu.SemaphoreType.DMA((STAGE_SLOTS,)),
            pltpu.VMEM((tm, D_MODEL), _BF16),
            pltpu.VMEM((D_CONV // LANES, HALO + tm, LANES), _F32),
            pltpu.VMEM((D_CONV // LANES, tm, LANES), _F32),
            pltpu.VMEM((tm, D_CONV), _BF16),
            pltpu.VMEM((D_XBC // LANES, HALO + tm, LANES), _F32),
            pltpu.VMEM((D_INNER // LANES, tm, LANES), _F32),
            pltpu.VMEM((2 * N_GROUPS, tm, D_STATE), _F32),
            pltpu.VMEM((tm, D_INNER), _F32),
            pltpu.VMEM((tm // CHUNK, CHUNK, HEADS_PAD), _F32),
            pltpu.VMEM((tm // CHUNK, HEADS_PAD, CHUNK), _F32),
            pltpu.VMEM((tm // CHUNK, CHUNK, HEADS_PAD), _F32),
            pltpu.VMEM((tm // CHUNK, CHUNK, HEADS_PAD), _F32),
            pltpu.VMEM((tm, D_INNER), _BF16),
            pltpu.VMEM((tm, D_MODEL), _F32),
            pltpu.VMEM((tm, D_MODEL), _F32),
            pltpu.VMEM((D_STATE, D_INNER), _F32),
        ],
        compiler_params=pltpu.CompilerParams(
            dimension_semantics=("arbitrary", "arbitrary"),
            vmem_limit_bytes=MIXER_VMEM_LIMIT),
        name="mixer",
    )(x, *params)


def _ffn_call(h, nw, wup, fcw, fcb, wdown, fnw):
    batch, seq, _ = h.shape
    tm = FFN_TILE
    params = (nw, wup, fcw, fcb, wdown, fnw)
    in_hbm = (wup, wdown)
    assert wup.shape == (D_MODEL, 2 * D_FF) and D_MODEL % UP_STAGE_ROWS == 0
    assert wdown.shape == (D_FF, D_MODEL) and D_FF % SQUARE_STAGE_ROWS == 0
    return pl.pallas_call(
        _ffn_kernel,
        out_shape=jax.ShapeDtypeStruct(h.shape, _F32),
        grid=(batch, seq // tm),
        in_specs=[_token_tile(tm)] + [_IN_HBM if any(p is w for w in in_hbm) else _resident(p.shape)
                                      for p in params],
        out_specs=_token_tile(tm),
        scratch_shapes=[
            pltpu.VMEM(wup.shape, _BF16),
            pltpu.VMEM(wdown.shape, _BF16),
            pltpu.VMEM((STAGE_SLOTS, UP_STAGE_ROWS, 2 * D_FF), _F32),
            pltpu.VMEM((STAGE_SLOTS, SQUARE_STAGE_ROWS, D_MODEL), _F32),
            pltpu.SemaphoreType.DMA((STAGE_SLOTS,)),
            pltpu.VMEM((D_FF // LANES, HALO + tm, LANES), _F32),
            pltpu.VMEM((D_FF // LANES, tm, LANES), _F32),
        ],
        compiler_params=pltpu.CompilerParams(
            dimension_semantics=("arbitrary", "arbitrary"),
            vmem_limit_bytes=FFN_VMEM_LIMIT),
        name="ffn",
    )(h, *params)


def _row(v, pad_to=None):
    v = v.astype(_F32).reshape(1, -1)
    if pad_to is not None:
        v = jnp.pad(v, ((0, 0), (0, pad_to - v.shape[1])))
    return v


def kernel(x, norm_mix_w, w_in, conv_a_w, w_a_out, ssd_conv_w, ssd_conv_b, dt_bias, a_log, d_skip,
           ssd_norm_w, w_s_out, w_o, norm_ffn_w, w_up, ffn_conv_w, ffn_conv_b, w_down, final_norm_w):
    depth = w_in.shape[0]
    assert x.shape[1] % MIXER_TILE == 0 and x.shape[1] % FFN_TILE == 0
    assert MIXER_TILE % CHUNK == 0
    assert MIXER_TILE % (CONV_STRIDE * SUBLANES) == 0 and FFN_TILE % (CONV_STRIDE * SUBLANES) == 0
    assert HEAD_DIM & (HEAD_DIM - 1) == 0
    h = x
    for l in range(depth):
        wdt = jnp.pad(w_in[l, :, OFF_DT:N_IN], ((0, 0), (0, HEADS_PAD - N_HEADS))).astype(_BF16)
        h = _mixer_call(
            h, _row(norm_mix_w[l]), jnp.swapaxes(w_in[l], 0, 1).astype(_F32), wdt, conv_a_w[l].astype(_F32),
            w_a_out[l].astype(_F32), ssd_conv_w[l].astype(_F32), _row(ssd_conv_b[l]),
            _row(dt_bias[l], HEADS_PAD), _row(a_log[l], HEADS_PAD), _row(jnp.repeat(d_skip[l], HEAD_DIM)),
            _row(ssd_norm_w[l]), w_s_out[l].astype(_F32), w_o[l].astype(_F32))
        assert depth == 1
        h = _ffn_call(h, _row(norm_ffn_w[l]), w_up[l].astype(_F32), ffn_conv_w[l].astype(_F32),
                      _row(ffn_conv_b[l]), w_down[l].astype(_F32), _row(final_norm_w))
    return h
```

```python
import jax
import jax.numpy as jnp
from jax import lax
from jax.experimental import pallas as pl
from jax.experimental.pallas import tpu as pltpu

D_MODEL = 1024
D_CONV = D_MODEL
CONV_A_WIDTH = 3
D_INNER = 2048
HEAD_DIM = 64
N_HEADS = 32
N_GROUPS = 4
D_STATE = 128
SSD_CONV_WIDTH = 4
CHUNK = 128
D_XBC = D_INNER + 2 * N_GROUPS * D_STATE
D_FF = 2816
FFN_CONV_WIDTH = 3
EPS = 1e-5
LOG2_E = 1.4426950408889634

LANES = 128
SUBLANES = 8
CONV_STRIDE = 4

HEADS_PAD = LANES
OFF_GATE_A = 0
OFF_GATE_S = OFF_GATE_A + D_MODEL
OFF_B_A = OFF_GATE_S + D_MODEL
OFF_C_A = OFF_B_A + D_CONV
OFF_V_A = OFF_C_A + D_CONV
OFF_Z = OFF_V_A + D_CONV
OFF_XBC = OFF_Z + D_INNER
OFF_DT = OFF_XBC + D_XBC
N_IN = OFF_DT + N_HEADS

HEADS_PER_GROUP = N_HEADS // N_GROUPS
GROUP_WIDTH = HEADS_PER_GROUP * HEAD_DIM
GROUP_SLABS = GROUP_WIDTH // LANES
HALO = SUBLANES

MIXER_TILE = 256
FFN_TILE = 512
UP_STAGE_ROWS = 64
SQUARE_STAGE_ROWS = 256
STAGE_SLOTS = 4
PROJ_BLOCK = 256
MIXER_VMEM_LIMIT = 60 * 1024 * 1024
FFN_VMEM_LIMIT = 56 * 1024 * 1024

_F32 = jnp.float32
_BF16 = jnp.bfloat16


def _dot(a, b):
    return jnp.dot(a, b, preferred_element_type=_F32)


def _rmsnorm(x, w):
    return x * lax.rsqrt(jnp.mean(x * x, axis=-1, keepdims=True) + EPS) * w


def _silu(x):
    return x * jax.nn.sigmoid(x)


def _to_slabs(buf_ref, value, row0):
    rows = value.shape[0]
    for s in range(value.shape[1] // LANES):
        buf_ref[s, row0:row0 + rows, :] = value[:, s * LANES:(s + 1) * LANES]


def _from_slabs(buf_ref, rows, slabs):
    return jnp.concatenate([buf_ref[s, rows, :] for s in slabs], axis=1)


def _causal_conv_slab(buf_ref, slab, w_ref, width, rows):
    n = rows // CONV_STRIDE
    first = HALO - (width - 1)
    reads = [buf_ref[slab, pl.ds(j, n, stride=CONV_STRIDE), :]
             for j in range(first, HALO + CONV_STRIDE)]
    cols = slice(slab * LANES, (slab + 1) * LANES)
    taps = [w_ref[k:k + 1, cols] for k in range(width)]
    outs = []
    for r in range(CONV_STRIDE):
        acc = taps[0] * reads[r]
        for k in range(1, width):
            acc = acc + taps[k] * reads[r + k]
        outs.append(acc)
    return outs


def _store_strided(dst_ref, slab, outs):
    n = outs[0].shape[0]
    for r, val in enumerate(outs):
        dst_ref[slab, pl.ds(r, n, stride=CONV_STRIDE), :] = val


def _keep_halo(buf_ref, rows):
    for s in range(buf_ref.shape[0]):
        buf_ref[s, 0:HALO, :] = buf_ref[s, rows:rows + HALO, :]


def _stream_blocks(src_hbm, n_blocks, stage_ref, sem_ref, consume):
    n_slots, block_rows = stage_ref.shape[0], stage_ref.shape[1]
    lookahead = n_slots - 1

    def block_copy(i, slot):
        return pltpu.make_async_copy(src_hbm.at[pl.ds(i * block_rows, block_rows), :],
                                     stage_ref.at[slot], sem_ref.at[slot])

    for i in range(min(lookahead, n_blocks)):
        block_copy(i, i % n_slots).start()

    def body(i, carry):
        @pl.when(i + lookahead < n_blocks)
        def _():
            block_copy(i + lookahead, lax.rem(i + lookahead, n_slots)).start()

        slot = lax.rem(i, n_slots)
        block_copy(i, slot).wait()
        consume(i, stage_ref[slot])
        return carry

    lax.fori_loop(0, n_blocks, body, 0)


def _stream_cast(src_hbm, dst_ref, stage_ref, sem_ref):
    block_rows = stage_ref.shape[1]

    def consume(i, block):
        r0 = pl.multiple_of(i * block_rows, block_rows)
        dst_ref[pl.ds(r0, block_rows), :] = block.astype(_BF16)

    _stream_blocks(src_hbm, dst_ref.shape[0] // block_rows, stage_ref, sem_ref, consume)


def _stream_cast_transposed(src_hbm, dst_ref, stage_ref, sem_ref):
    def consume(i, block):
        dst_ref[i] = block.T.astype(_BF16)

    _stream_blocks(src_hbm, dst_ref.shape[0], stage_ref, sem_ref, consume)


def _first_step():
    return jnp.logical_and(pl.program_id(0) == 0, pl.program_id(1) == 0)


def _cumsum_rows(x, row_id):
    shift = 1
    while shift < x.shape[0]:
        x = x + jnp.where(row_id >= shift, pltpu.roll(x, shift, axis=0), 0.0)
        shift *= 2
    return x


def _mixer_kernel(x_ref, nw_ref, win_hbm, wdt_ref, caw_ref, waout_hbm, scw_ref, scb_ref, dtb_ref, alog_ref,
                  dskip_ref, snw_ref, wsout_hbm, wo_hbm,
                  o_ref,
                  win_ref, waout_ref, wsout_ref, wo_ref, square_stage_ref, stage_sem,
                  u_ref, cv_ref, conva_ref, yain_ref, xbc_ref, xs_ref, bc_ref, z_ref, acum_ref,
                  acumt_ref, dtend_ref, decin_ref, yn_ref, merged_ref, gs_ref, state_ref):
    tm = x_ref.shape[0]
    n_chunks = tm // CHUNK
    n_xs_slabs = D_INNER // LANES

    @pl.when(_first_step())
    def _():
        _stream_cast_transposed(win_hbm, win_ref, square_stage_ref, stage_sem)
        _stream_cast(waout_hbm, waout_ref, square_stage_ref, stage_sem)
        _stream_cast(wsout_hbm, wsout_ref, square_stage_ref, stage_sem)
        _stream_cast(wo_hbm, wo_ref, square_stage_ref, stage_sem)

    @pl.when(pl.program_id(1) == 0)
    def _():
        cv_ref[:, 0:HALO, :] = jnp.zeros((cv_ref.shape[0], HALO, LANES), _F32)
        xbc_ref[:, 0:HALO, :] = jnp.zeros((xbc_ref.shape[0], HALO, LANES), _F32)
        state_ref[...] = jnp.zeros(state_ref.shape, _F32)

    u_ref[...] = _rmsnorm(x_ref[...], nw_ref[...]).astype(_BF16)

    def proj(off, width):
        blocks = range(off // PROJ_BLOCK, (off + width) // PROJ_BLOCK)
        return jnp.concatenate([_dot(u_ref[...], win_ref[j]) for j in blocks], axis=1)

    a_row = -jnp.exp(alog_ref[...])
    row_id = lax.broadcasted_iota(jnp.int32, (CHUNK, CHUNK), 0)
    col_id = lax.broadcasted_iota(jnp.int32, (CHUNK, CHUNK), 1)
    causal = row_id >= col_id
    low_half = lax.broadcasted_iota(jnp.int32, (CHUNK, LANES), 1) < HEAD_DIM

    _to_slabs(xbc_ref, proj(OFF_XBC, D_XBC), HALO)

    dt_raw = _dot(u_ref[...], wdt_ref[...]) + dtb_ref[...]
    z_ref[...] = _silu(proj(OFF_Z, D_INNER))
    dt_all =jnp.maximum(dt_raw, 0.0) + jnp.log1p(jnp.exp(-jnp.abs(dt_raw)))
    for ci in range(n_chunks):
        dt = dt_all[ci * CHUNK:(ci + 1) * CHUNK, :]
        a_cum = _cumsum_rows(dt * a_row, row_id)
        a2 = a_cum * LOG2_E
        acum_ref[ci] = a2
        acumt_ref[ci] = (a2 - jnp.log(dt) * LOG2_E).T
        decay_to_end = jnp.exp2(a2[CHUNK - 1:CHUNK, :] - a2)
        decay_in = jnp.exp2(a2)
        dtend_ref[ci] = dt * decay_to_end
        decin_ref[ci] = decay_in

    for s in range(D_XBC // LANES):
        cols = slice(s * LANES, (s + 1) * LANES)
        outs = [_silu(o + scb_ref[:, cols])
                for o in _causal_conv_slab(xbc_ref, s, scw_ref, SSD_CONV_WIDTH, tm)]
        if s < n_xs_slabs:
            _store_strided(xs_ref, s, outs)
        else:
            _store_strided(bc_ref, s - n_xs_slabs, outs)
    _keep_halo(xbc_ref, tm)

    a_slabs = range(D_CONV // LANES)

    def step_c():
        _to_slabs(conva_ref, proj(OFF_C_A, D_CONV), 0)

    def step_cv():
        _to_slabs(cv_ref, proj(OFF_V_A, D_CONV) * _from_slabs(conva_ref, slice(0, tm), a_slabs), HALO)

    def step_conv_b():
        b_a = proj(OFF_B_A, D_CONV)
        for s in a_slabs:
            _store_strided(conva_ref, s, _causal_conv_slab(cv_ref, s, caw_ref, CONV_A_WIDTH, tm))
        _keep_halo(cv_ref, tm)
        yain_ref[...] = (b_a * _from_slabs(conva_ref, slice(0, tm), a_slabs)).astype(_BF16)

    def step_ya():
        merged_ref[...] = _dot(yain_ref[...], waout_ref[...])

    def step_gate_a():
        merged_ref[...] = jax.nn.sigmoid(proj(OFF_GATE_A, D_MODEL)) * merged_ref[...]

    def step_gate_s():
        gs_ref[...] = jax.nn.sigmoid(proj(OFF_GATE_S, D_MODEL))

    def step_merge(rows):
        y_s = _dot(yn_ref[rows, :], wsout_ref[...])
        yain_ref[rows, :] = (merged_ref[rows, :] + gs_ref[rows, :] * y_s).astype(_BF16)

    def step_out(rows):
        o_ref[rows, :] = x_ref[rows, :] + _dot(yain_ref[rows, :], wo_ref[...])

    dense_steps = [step_c, step_cv, step_conv_b, step_ya, step_gate_a, step_gate_s]

    def group_prelude(ci, g):
        rows = slice(ci * CHUNK, (ci + 1) * CHUNK)
        gcols = slice(g * GROUP_WIDTH, (g + 1) * GROUP_WIDTH)
        x_end_cols, decay_in_cols = [], []
        for jp in range(GROUP_SLABS):
            h0 = g * HEADS_PER_GROUP + 2 * jp
            to_end = jnp.where(low_half, dtend_ref[ci, :, h0:h0 + 1], dtend_ref[ci, :, h0 + 1:h0 + 2])
            x_end_cols.append((xs_ref[g * GROUP_SLABS + jp, rows, :] * to_end).astype(_BF16))
            decay_in_cols.append(jnp.where(low_half, decin_ref[ci, :, h0:h0 + 1],
                                           decin_ref[ci, :, h0 + 1:h0 + 2]))
        x_end = jnp.concatenate(x_end_cols, axis=1)
        decay_in_full = jnp.concatenate(decay_in_cols, axis=1)
        b_g = bc_ref[g, rows, :].astype(_BF16)
        c_g = bc_ref[N_GROUPS + g, rows, :].astype(_BF16)
        cb = lax.dot_general(c_g, b_g, (((1,), (1,)), ((), ())), preferred_element_type=_F32)
        state_prev = state_ref[:, gcols]
        y_off = _dot(c_g, state_prev.astype(_BF16)) * decay_in_full
        if dense_steps:
            dense_steps.pop(0)()
        state_new = lax.dot_general(b_g, x_end, (((0,), (0,)), ((), ())), preferred_element_type=_F32)
        state_ref[:, gcols] = state_prev * decay_in_full[CHUNK - 1:CHUNK, :] + state_new
        return cb, y_off

    def group_body(ci, g, cb, y_off):
        rows = slice(ci * CHUNK, (ci + 1) * CHUNK)
        gcols = slice(g * GROUP_WIDTH, (g + 1) * GROUP_WIDTH)
        y_cols = []
        for jp in range(GROUP_SLABS):
            h0 = g * HEADS_PER_GROUP + 2 * jp
            pcols = slice(h0 * HEAD_DIM, (h0 + 2) * HEAD_DIM)
            lcols = slice(jp * LANES, (jp + 1) * LANES)
            decays = []
            for h in (h0, h0 + 1):
                seg = acum_ref[ci, :, h:h + 1] - acumt_ref[ci, h:h + 1, :]
                decays.append((cb * jnp.exp2(jnp.where(causal, seg, -jnp.inf))).astype(_BF16))
            lhs = jnp.concatenate(decays, axis=1)
            x_f32 = xs_ref[g * GROUP_SLABS + jp, rows, :]
            x_pair = x_f32.astype(_BF16)
            zero = jnp.zeros_like(x_pair)
            rhs = jnp.concatenate([jnp.where(low_half, x_pair, zero),
                                   jnp.where(low_half, zero, x_pair)], axis=0)
            y_cols.append(_dot(lhs, rhs) + y_off[:, lcols] + dskip_ref[:, pcols] * x_f32)
        yz = jnp.concatenate(y_cols, axis=1) * z_ref[rows, gcols]
        yn = yz * lax.rsqrt(jnp.mean(yz * yz, axis=-1, keepdims=True) + EPS) * snw_ref[:, gcols]
        yn_ref[rows, gcols] = yn.astype(_BF16)

    order = [(ci, g) for ci in range(n_chunks) for g in range(N_GROUPS)]
    ahead = group_prelude(*order[0])
    for k, (ci, g) in enumerate(order):
        current = ahead
        if k + 1 < len(order):
            ahead = group_prelude(*order[k + 1])
        group_body(ci, g, *current)
    while dense_steps:
        dense_steps.pop(0)()
    step_merge(slice(0, tm))
    step_out(slice(0, tm))


def _ffn_kernel(h_ref, nw_ref, wup_hbm, fcw_ref, fcb_ref, wdown_hbm, fnw_ref, o_ref,
                wup_ref, wdown_ref, up_stage_ref, square_stage_ref, stage_sem, h1_ref, act_ref):
    tm = h_ref.shape[0]

    @pl.when(_first_step())
    def _():
        _stream_cast(wup_hbm, wup_ref, up_stage_ref, stage_sem)
        _stream_cast(wdown_hbm, wdown_ref, square_stage_ref, stage_sem)

    @pl.when(pl.program_id(1) == 0)
    def _():
        h1_ref[:, 0:HALO, :] = jnp.zeros((h1_ref.shape[0], HALO, LANES), _F32)

    h = h_ref[...]
    v = _rmsnorm(h, nw_ref[...]).astype(_BF16)
    _to_slabs(h1_ref, _dot(v, wup_ref[:, 0:D_FF]), HALO)
    h3 = _dot(v, wup_ref[:, D_FF:2 * D_FF])
    for s in range(D_FF // LANES):
        cols = slice(s * LANES, (s + 1) * LANES)
        outs = [_silu(o + fcb_ref[:, cols])
                for o in _causal_conv_slab(h1_ref, s, fcw_ref, FFN_CONV_WIDTH, tm)]
        _store_strided(act_ref, s, outs)
    _keep_halo(h1_ref, tm)
    act = _from_slabs(act_ref, slice(0, tm), range(D_FF // LANES))
    out = h + _dot((act * h3).astype(_BF16), wdown_ref[...])
    o_ref[...] = _rmsnorm(out, fnw_ref[...])


def _resident(shape):
    return pl.BlockSpec(shape, lambda b, s: (0,) * len(shape), pipeline_mode=pl.Buffered(1))


def _token_tile(tile):
    return pl.BlockSpec((None, tile, D_MODEL), lambda b, s: (b, s, 0))


_IN_HBM = pl.BlockSpec(memory_space=pl.ANY)


def _mixer_call(x, nw, win, wdt, caw, waout, scw, scb, dtb, alog, dskip, snw, wsout, wo):
    batch, seq, _ = x.shape
    tm = MIXER_TILE
    params = (nw, win, wdt, caw, waout, scw, scb, dtb, alog, dskip, snw, wsout, wo)
    in_hbm = (win, waout, wsout, wo)
    assert win.shape == (N_IN, D_MODEL) and PROJ_BLOCK == SQUARE_STAGE_ROWS and OFF_DT % PROJ_BLOCK == 0
    assert all(w.shape[0] % SQUARE_STAGE_ROWS == 0 and w.shape[1] == D_MODEL for w in (waout, wsout, wo))
    return pl.pallas_call(
        _mixer_kernel,
        out_shape=jax.ShapeDtypeStruct(x.shape, _F32),
        grid=(batch, seq // tm),
        in_specs=[_token_tile(tm)] + [_IN_HBM if any(p is w for w in in_hbm) else _resident(p.shape)
                                      for p in params],
        out_specs=_token_tile(tm),
        scratch_shapes=[
            pltpu.VMEM((OFF_DT // PROJ_BLOCK, D_MODEL, PROJ_BLOCK), _BF16),
            pltpu.VMEM(waout.shape, _BF16),
            pltpu.VMEM(wsout.shape, _BF16),
            pltpu.VMEM(wo.shape, _BF16),
            pltpu.VMEM((STAGE_SLOTS, SQUARE_STAGE_ROWS, D_MODEL), _F32),
            pltpu.SemaphoreType.DMA((STAGE_SLOTS,)),
            pltpu.VMEM((tm, D_MODEL), _BF16),
            pltpu.VMEM((D_CONV // LANES, HALO + tm, LANES), _F32),
            pltpu.VMEM((D_CONV // LANES, tm, LANES), _F32),
            pltpu.VMEM((tm, D_CONV), _BF16),
            pltpu.VMEM((D_XBC // LANES, HALO + tm, LANES), _F32),
            pltpu.VMEM((D_INNER // LANES, tm, LANES), _F32),
            pltpu.VMEM((2 * N_GROUPS, tm, D_STATE), _F32),
            pltpu.VMEM((tm, D_INNER), _F32),
            pltpu.VMEM((tm // CHUNK, CHUNK, HEADS_PAD), _F32),
            pltpu.VMEM((tm // CHUNK, HEADS_PAD, CHUNK), _F32),
            pltpu.VMEM((tm // CHUNK, CHUNK, HEADS_PAD), _F32),
            pltpu.VMEM((tm // CHUNK, CHUNK, HEADS_PAD), _F32),
            pltpu.VMEM((tm, D_INNER), _BF16),
            pltpu.VMEM((tm, D_MODEL), _F32),
            pltpu.VMEM((tm, D_MODEL), _F32),
            pltpu.VMEM((D_STATE, D_INNER), _F32),
        ],
        compiler_params=pltpu.CompilerParams(
            dimension_semantics=("arbitrary", "arbitrary"),
            vmem_limit_bytes=MIXER_VMEM_LIMIT),
        name="mixer",
    )(x, *params)


def _ffn_call(h, nw, wup, fcw, fcb, wdown, fnw):
    batch, seq, _ = h.shape
    tm = FFN_TILE
    params = (nw, wup, fcw, fcb, wdown, fnw)
    in_hbm = (wup, wdown)
    assert wup.shape == (D_MODEL, 2 * D_FF) and D_MODEL % UP_STAGE_ROWS == 0
    assert wdown.shape == (D_FF, D_MODEL) and D_FF % SQUARE_STAGE_ROWS == 0
    return pl.pallas_call(
        _ffn_kernel,
        out_shape=jax.ShapeDtypeStruct(h.shape, _F32),
        grid=(batch, seq // tm),
        in_specs=[_token_tile(tm)] + [_IN_HBM if any(p is w for w in in_hbm) else _resident(p.shape)
                                      for p in params],
        out_specs=_token_tile(tm),
        scratch_shapes=[
            pltpu.VMEM(wup.shape, _BF16),
            pltpu.VMEM(wdown.shape, _BF16),
            pltpu.VMEM((STAGE_SLOTS, UP_STAGE_ROWS, 2 * D_FF), _F32),
            pltpu.VMEM((STAGE_SLOTS, SQUARE_STAGE_ROWS, D_MODEL), _F32),
            pltpu.SemaphoreType.DMA((STAGE_SLOTS,)),
            pltpu.VMEM((D_FF // LANES, HALO + tm, LANES), _F32),
            pltpu.VMEM((D_FF // LANES, tm, LANES), _F32),
        ],
        compiler_params=pltpu.CompilerParams(
            dimension_semantics=("arbitrary", "arbitrary"),
            vmem_limit_bytes=FFN_VMEM_LIMIT),
        name="ffn",
    )(h, *params)


def _row(v, pad_to=None):
    v = v.astype(_F32).reshape(1, -1)
    if pad_to is not None:
        v = jnp.pad(v, ((0, 0), (0, pad_to - v.shape[1])))
    return v


def kernel(x, norm_mix_w, w_in, conv_a_w, w_a_out, ssd_conv_w, ssd_conv_b, dt_bias, a_log, d_skip,
           ssd_norm_w, w_s_out, w_o, norm_ffn_w, w_up, ffn_conv_w, ffn_conv_b, w_down, final_norm_w):
    depth = w_in.shape[0]
    assert x.shape[1] % MIXER_TILE == 0 and x.shape[1] % FFN_TILE == 0
    assert MIXER_TILE % CHUNK == 0
    assert MIXER_TILE % (CONV_STRIDE * SUBLANES) == 0 and FFN_TILE % (CONV_STRIDE * SUBLANES) == 0
    assert HEAD_DIM & (HEAD_DIM - 1) == 0
    h = x
    for l in range(depth):
        wdt = jnp.pad(w_in[l, :, OFF_DT:N_IN], ((0, 0), (0, HEADS_PAD - N_HEADS))).astype(_BF16)
        h = _mixer_call(
            h, _row(norm_mix_w[l]), jnp.swapaxes(w_in[l], 0, 1).astype(_F32), wdt, conv_a_w[l].astype(_F32),
            w_a_out[l].astype(_F32), ssd_conv_w[l].astype(_F32), _row(ssd_conv_b[l]),
            _row(dt_bias[l], HEADS_PAD), _row(a_log[l], HEADS_PAD), _row(jnp.repeat(d_skip[l], HEAD_DIM)),
            _row(ssd_norm_w[l]), w_s_out[l].astype(_F32), w_o[l].astype(_F32))
        assert depth == 1
        h = _ffn_call(h, _row(norm_ffn_w[l]), w_up[l].astype(_F32), ffn_conv_w[l].astype(_F32),
                      _row(ffn_conv_b[l]), w_down[l].astype(_F32), _row(final_norm_w))
    return h
```

```python
import jax
import jax.numpy as jnp
from jax import lax
from jax.experimental import pallas as pl
from jax.experimental.pallas import tpu as pltpu

D_MODEL = 1024
D_CONV = D_MODEL
CONV_A_WIDTH = 3
D_INNER = 2048
HEAD_DIM = 64
N_HEADS = 32
N_GROUPS = 4
D_STATE = 128
SSD_CONV_WIDTH = 4
CHUNK = 128
D_XBC = D_INNER + 2 * N_GROUPS * D_STATE
D_FF = 2816
FFN_CONV_WIDTH = 3
EPS = 1e-5
LOG2_E = 1.4426950408889634

LANES = 128
SUBLANES = 8
CONV_STRIDE = 4

HEADS_PAD = LANES
OFF_GATE_A = 0
OFF_GATE_S = OFF_GATE_A + D_MODEL
OFF_B_A = OFF_GATE_S + D_MODEL
OFF_C_A = OFF_B_A + D_CONV
OFF_V_A = OFF_C_A + D_CONV
OFF_Z = OFF_V_A + D_CONV
OFF_XBC = OFF_Z + D_INNER
OFF_DT = OFF_XBC + D_XBC
N_IN = OFF_DT + N_HEADS

HEADS_PER_GROUP = N_HEADS // N_GROUPS
GROUP_WIDTH = HEADS_PER_GROUP * HEAD_DIM
GROUP_SLABS = GROUP_WIDTH // LANES
HALO = SUBLANES

MIXER_TILE = 256
FFN_TILE = 512
UP_STAGE_ROWS = 64
SQUARE_STAGE_ROWS = 256
STAGE_SLOTS = 4
PROJ_BLOCK = 256
MIXER_VMEM_LIMIT = 60 * 1024 * 1024
FFN_VMEM_LIMIT = 56 * 1024 * 1024

_F32 = jnp.float32
_BF16 = jnp.bfloat16


def _dot(a, b):
    return jnp.dot(a, b, preferred_element_type=_F32)


def _rmsnorm(x, w):
    return x * lax.rsqrt(jnp.mean(x * x, axis=-1, keepdims=True) + EPS) * w


def _silu(x):
    return x * jax.nn.sigmoid(x)


def _to_slabs(buf_ref, value, row0):
    rows = value.shape[0]
    for s in range(value.shape[1] // LANES):
        buf_ref[s, row0:row0 + rows, :] = value[:, s * LANES:(s + 1) * LANES]


def _from_slabs(buf_ref, rows, slabs):
    return jnp.concatenate([buf_ref[s, rows, :] for s in slabs], axis=1)


def _causal_conv_slab(buf_ref, slab, w_ref, width, rows):
    n = rows // CONV_STRIDE
    first = HALO - (width - 1)
    reads = [buf_ref[slab, pl.ds(j, n, stride=CONV_STRIDE), :]
             for j in range(first, HALO + CONV_STRIDE)]
    cols = slice(slab * LANES, (slab + 1) * LANES)
    taps = [w_ref[k:k + 1, cols] for k in range(width)]
    outs = []
    for r in range(CONV_STRIDE):
        acc = taps[0] * reads[r]
        for k in range(1, width):
            acc = acc + taps[k] * reads[r + k]
        outs.append(acc)
    return outs


def _store_strided(dst_ref, slab, outs):
    n = outs[0].shape[0]
    for r, val in enumerate(outs):
        dst_ref[slab, pl.ds(r, n, stride=CONV_STRIDE), :] = val


def _keep_halo(buf_ref, rows):
    for s in range(buf_ref.shape[0]):
        buf_ref[s, 0:HALO, :] = buf_ref[s, rows:rows + HALO, :]


def _stream_blocks(src_hbm, n_blocks, stage_ref, sem_ref, consume):
    n_slots, block_rows = stage_ref.shape[0], stage_ref.shape[1]
    lookahead = n_slots - 1

    def block_copy(i, slot):
        return pltpu.make_async_copy(src_hbm.at[pl.ds(i * block_rows, block_rows), :],
                                     stage_ref.at[slot], sem_ref.at[slot])

    for i in range(min(lookahead, n_blocks)):
        block_copy(i, i % n_slots).start()

    def body(i, carry):
        @pl.when(i + lookahead < n_blocks)
        def _():
            block_copy(i + lookahead, lax.rem(i + lookahead, n_slots)).start()

        slot = lax.rem(i, n_slots)
        block_copy(i, slot).wait()
        consume(i, stage_ref[slot])
        return carry

    lax.fori_loop(0, n_blocks, body, 0)


def _stream_cast(src_hbm, dst_ref, stage_ref, sem_ref):
    block_rows = stage_ref.shape[1]

    def consume(i, block):
        r0 = pl.multiple_of(i * block_rows, block_rows)
        dst_ref[pl.ds(r0, block_rows), :] = block.astype(_BF16)

    _stream_blocks(src_hbm, dst_ref.shape[0] // block_rows, stage_ref, sem_ref, consume)


def _stream_cast_transposed(src_hbm, dst_ref, stage_ref, sem_ref):
    def consume(i, block):
        dst_ref[i] = block.T.astype(_BF16)

    _stream_blocks(src_hbm, dst_ref.shape[0], stage_ref, sem_ref, consume)


def _first_step():
    return jnp.logical_and(pl.program_id(0) == 0, pl.program_id(1) == 0)


def _cumsum_rows(x, row_id):
    shift = 1
    while shift < x.shape[0]:
        x = x + jnp.where(row_id >= shift, pltpu.roll(x, shift, axis=0), 0.0)
        shift *= 2
    return x


def _mixer_kernel(x_ref, nw_ref, win_hbm, wdt_ref, caw_ref, waout_hbm, scw_ref, scb_ref, dtb_ref, alog_ref,
                  dskip_ref, snw_ref, wsout_hbm, wo_hbm,
                  o_ref,
                  win_ref, waout_ref, wsout_ref, wo_ref, square_stage_ref, stage_sem,
                  u_ref, cv_ref, conva_ref, yain_ref, xbc_ref, xs_ref, bc_ref, z_ref, acum_ref,
                  acumt_ref, dtend_ref, decin_ref, yn_ref, merged_ref, gs_ref, state_ref):
    tm = x_ref.shape[0]
    n_chunks = tm // CHUNK
    n_xs_slabs = D_INNER // LANES

    @pl.when(_first_step())
    def _():
        _stream_cast_transposed(win_hbm, win_ref, square_stage_ref, stage_sem)
        _stream_cast(waout_hbm, waout_ref, square_stage_ref, stage_sem)
        _stream_cast(wsout_hbm, wsout_ref, square_stage_ref, stage_sem)
        _stream_cast(wo_hbm, wo_ref, square_stage_ref, stage_sem)

    @pl.when(pl.program_id(1) == 0)
    def _():
        cv_ref[:, 0:HALO, :] = jnp.zeros((cv_ref.shape[0], HALO, LANES), _F32)
        xbc_ref[:, 0:HALO, :] = jnp.zeros((xbc_ref.shape[0], HALO, LANES), _F32)
        state_ref[...] = jnp.zeros(state_ref.shape, _F32)

    u_ref[...] = _rmsnorm(x_ref[...], nw_ref[...]).astype(_BF16)

    def proj(off, width):
        blocks = range(off // PROJ_BLOCK, (off + width) // PROJ_BLOCK)
        return jnp.concatenate([_dot(u_ref[...], win_ref[j]) for j in blocks], axis=1)

    a_row = -jnp.exp(alog_ref[...])
    row_id = lax.broadcasted_iota(jnp.int32, (CHUNK, CHUNK), 0)
    col_id = lax.broadcasted_iota(jnp.int32, (CHUNK, CHUNK), 1)
    causal = row_id >= col_id
    low_half = lax.broadcasted_iota(jnp.int32, (CHUNK, LANES), 1) < HEAD_DIM

    _to_slabs(xbc_ref, proj(OFF_XBC, D_XBC), HALO)

    dt_raw = _dot(u_ref[...], wdt_ref[...]) + dtb_ref[...]
    z_ref[...] = _silu(proj(OFF_Z, D_INNER))
    dt_all =jnp.maximum(dt_raw, 0.0) + jnp.log1p(jnp.exp(-jnp.abs(dt_raw)))
    for ci in range(n_chunks):
        dt = dt_all[ci * CHUNK:(ci + 1) * CHUNK, :]
        a_cum = _cumsum_rows(dt * a_row, row_id)
        a2 = a_cum * LOG2_E
        acum_ref[ci] = a2
        acumt_ref[ci] = (a2 - jnp.log(dt) * LOG2_E).T
        decay_to_end = jnp.exp2(a2[CHUNK - 1:CHUNK, :] - a2)
        decay_in = jnp.exp2(a2)
        dtend_ref[ci] = dt * decay_to_end
        decin_ref[ci] = decay_in

    for s in range(D_XBC // LANES):
        cols = slice(s * LANES, (s + 1) * LANES)
        outs = [_silu(o + scb_ref[:, cols])
                for o in _causal_conv_slab(xbc_ref, s, scw_ref, SSD_CONV_WIDTH, tm)]
        if s < n_xs_slabs:
            _store_strided(xs_ref, s, outs)
        else:
            _store_strided(bc_ref, s - n_xs_slabs, outs)
    _keep_halo(xbc_ref, tm)

    a_slabs = range(D_CONV // LANES)

    def step_c():
        _to_slabs(conva_ref, proj(OFF_C_A, D_CONV), 0)

    def step_cv():
        _to_slabs(cv_ref, proj(OFF_V_A, D_CONV) * _from_slabs(conva_ref, slice(0, tm), a_slabs), HALO)

    def step_conv_b():
        b_a = proj(OFF_B_A, D_CONV)
        for s in a_slabs:
            _store_strided(conva_ref, s, _causal_conv_slab(cv_ref, s, caw_ref, CONV_A_WIDTH, tm))
        _keep_halo(cv_ref, tm)
        yain_ref[...] = (b_a * _from_slabs(conva_ref, slice(0, tm), a_slabs)).astype(_BF16)

    def step_ya():
        merged_ref[...] = _dot(yain_ref[...], waout_ref[...])

    def step_gate_a(cols):
        gate = jax.nn.sigmoid(proj(OFF_GATE_A + cols.start, cols.stop - cols.start))
        merged_ref[:, cols] = gate * merged_ref[:, cols]

    def step_gate_s(cols):
        gs_ref[:, cols] = jax.nn.sigmoid(proj(OFF_GATE_S + cols.start, cols.stop - cols.start))

    def step_merge(rows):
        y_s = _dot(yn_ref[rows, :], wsout_ref[...])
        yain_ref[rows, :] = (merged_ref[rows, :] + gs_ref[rows, :] * y_s).astype(_BF16)

    def step_out(rows):
        o_ref[rows, :] = x_ref[rows, :] + _dot(yain_ref[rows, :], wo_ref[...])

    halves = [slice(0, D_MODEL // 2), slice(D_MODEL // 2, D_MODEL)]
    dense_steps = ([step_c, step_cv, step_conv_b, step_ya]
                   + [lambda cols=cols: step_gate_a(cols) for cols in halves]
                   + [lambda cols=cols: step_gate_s(cols) for cols in halves])

    def group_prelude(ci, g):
        rows = slice(ci * CHUNK, (ci + 1) * CHUNK)
        gcols = slice(g * GROUP_WIDTH, (g + 1) * GROUP_WIDTH)
        x_end_cols, decay_in_cols = [], []
        for jp in range(GROUP_SLABS):
            h0 = g * HEADS_PER_GROUP + 2 * jp
            to_end = jnp.where(low_half, dtend_ref[ci, :, h0:h0 + 1], dtend_ref[ci, :, h0 + 1:h0 + 2])
            x_end_cols.append((xs_ref[g * GROUP_SLABS + jp, rows, :] * to_end).astype(_BF16))
            decay_in_cols.append(jnp.where(low_half, decin_ref[ci, :, h0:h0 + 1],
                                           decin_ref[ci, :, h0 + 1:h0 + 2]))
        x_end = jnp.concatenate(x_end_cols, axis=1)
        decay_in_full = jnp.concatenate(decay_in_cols, axis=1)
        b_g = bc_ref[g, rows, :].astype(_BF16)
        c_g = bc_ref[N_GROUPS + g, rows, :].astype(_BF16)
        cb = lax.dot_general(c_g, b_g, (((1,), (1,)), ((), ())), preferred_element_type=_F32)
        state_prev = state_ref[:, gcols]
        y_off = _dot(c_g, state_prev.astype(_BF16)) * decay_in_full
        if dense_steps:
            dense_steps.pop(0)()
        state_new = lax.dot_general(b_g, x_end, (((0,), (0,)), ((), ())), preferred_element_type=_F32)
        state_ref[:, gcols] = state_prev * decay_in_full[CHUNK - 1:CHUNK, :] + state_new
        return cb, y_off

    def group_body(ci, g, cb, y_off):
        rows = slice(ci * CHUNK, (ci + 1) * CHUNK)
        gcols = slice(g * GROUP_WIDTH, (g + 1) * GROUP_WIDTH)
        y_cols = []
        for jp in range(GROUP_SLABS):
            h0 = g * HEADS_PER_GROUP + 2 * jp
            pcols = slice(h0 * HEAD_DIM, (h0 + 2) * HEAD_DIM)
            lcols = slice(jp * LANES, (jp + 1) * LANES)
            decays = []
            for h in (h0, h0 + 1):
                seg = acum_ref[ci, :, h:h + 1] - acumt_ref[ci, h:h + 1, :]
                decays.append((cb * jnp.exp2(jnp.where(causal, seg, -jnp.inf))).astype(_BF16))
            lhs = jnp.concatenate(decays, axis=1)
            x_f32 = xs_ref[g * GROUP_SLABS + jp, rows, :]
            x_pair = x_f32.astype(_BF16)
            zero = jnp.zeros_like(x_pair)
            rhs = jnp.concatenate([jnp.where(low_half, x_pair, zero),
                                   jnp.where(low_half, zero, x_pair)], axis=0)
            y_cols.append(_dot(lhs, rhs) + y_off[:, lcols] + dskip_ref[:, pcols] * x_f32)
        yz = jnp.concatenate(y_cols, axis=1) * z_ref[rows, gcols]
        yn = yz * lax.rsqrt(jnp.mean(yz * yz, axis=-1, keepdims=True) + EPS) * snw_ref[:, gcols]
        yn_ref[rows, gcols] = yn.astype(_BF16)

    order = [(ci, g) for ci in range(n_chunks) for g in range(N_GROUPS)]
    ahead = group_prelude(*order[0])
    for k, (ci, g) in enumerate(order):
        current = ahead
        if k + 1 < len(order):
            ahead = group_prelude(*order[k + 1])
        group_body(ci, g, *current)
    while dense_steps:
        dense_steps.pop(0)()
    step_merge(slice(0, tm))
    step_out(slice(0, tm))


def _ffn_kernel(h_ref, nw_ref, wup_hbm, fcw_ref, fcb_ref, wdown_hbm, fnw_ref, o_ref,
                wup_ref, wdown_ref, up_stage_ref, square_stage_ref, stage_sem, h1_ref, act_ref):
    tm = h_ref.shape[0]

    @pl.when(_first_step())
    def _():
        _stream_cast(wup_hbm, wup_ref, up_stage_ref, stage_sem)
        _stream_cast(wdown_hbm, wdown_ref, square_stage_ref, stage_sem)

    @pl.when(pl.program_id(1) == 0)
    def _():
        h1_ref[:, 0:HALO, :] = jnp.zeros((h1_ref.shape[0], HALO, LANES), _F32)

    h = h_ref[...]
    v = _rmsnorm(h, nw_ref[...]).astype(_BF16)
    _to_slabs(h1_ref, _dot(v, wup_ref[:, 0:D_FF]), HALO)
    h3 = _dot(v, wup_ref[:, D_FF:2 * D_FF])
    for s in range(D_FF // LANES):
        cols = slice(s * LANES, (s + 1) * LANES)
        outs = [_silu(o + fcb_ref[:, cols])
                for o in _causal_conv_slab(h1_ref, s, fcw_ref, FFN_CONV_WIDTH, tm)]
        _store_strided(act_ref, s, outs)
    _keep_halo(h1_ref, tm)
    act = _from_slabs(act_ref, slice(0, tm), range(D_FF // LANES))
    out = h + _dot((act * h3).astype(_BF16), wdown_ref[...])
    o_ref[...] = _rmsnorm(out, fnw_ref[...])


def _resident(shape):
    return pl.BlockSpec(shape, lambda b, s: (0,) * len(shape), pipeline_mode=pl.Buffered(1))


def _token_tile(tile):
    return pl.BlockSpec((None, tile, D_MODEL), lambda b, s: (b, s, 0))


_IN_HBM = pl.BlockSpec(memory_space=pl.ANY)


def _mixer_call(x, nw, win, wdt, caw, waout, scw, scb, dtb, alog, dskip, snw, wsout, wo):
    batch, seq, _ = x.shape
    tm = MIXER_TILE
    params = (nw, win, wdt, caw, waout, scw, scb, dtb, alog, dskip, snw, wsout, wo)
    in_hbm = (win, waout, wsout, wo)
    assert win.shape == (N_IN, D_MODEL) and PROJ_BLOCK == SQUARE_STAGE_ROWS and OFF_DT % PROJ_BLOCK == 0
    assert all(w.shape[0] % SQUARE_STAGE_ROWS == 0 and w.shape[1] == D_MODEL for w in (waout, wsout, wo))
    return pl.pallas_call(
        _mixer_kernel,
        out_shape=jax.ShapeDtypeStruct(x.shape, _F32),
        grid=(batch, seq // tm),
        in_specs=[_token_tile(tm)] + [_IN_HBM if any(p is w for w in in_hbm) else _resident(p.shape)
                                      for p in params],
        out_specs=_token_tile(tm),
        scratch_shapes=[
            pltpu.VMEM((OFF_DT // PROJ_BLOCK, D_MODEL, PROJ_BLOCK), _BF16),
            pltpu.VMEM(waout.shape, _BF16),
            pltpu.VMEM(wsout.shape, _BF16),
            pltpu.VMEM(wo.shape, _BF16),
            pltpu.VMEM((STAGE_SLOTS, SQUARE_STAGE_ROWS, D_MODEL), _F32),
            pltpu.SemaphoreType.DMA((STAGE_SLOTS,)),
            pltpu.VMEM((tm, D_MODEL), _BF16),
            pltpu.VMEM((D_CONV // LANES, HALO + tm, LANES), _F32),
            pltpu.VMEM((D_CONV // LANES, tm, LANES), _F32),
            pltpu.VMEM((tm, D_CONV), _BF16),
            pltpu.VMEM((D_XBC // LANES, HALO + tm, LANES), _F32),
            pltpu.VMEM((D_INNER // LANES, tm, LANES), _F32),
            pltpu.VMEM((2 * N_GROUPS, tm, D_STATE), _F32),
            pltpu.VMEM((tm, D_INNER), _F32),
            pltpu.VMEM((tm // CHUNK, CHUNK, HEADS_PAD), _F32),
            pltpu.VMEM((tm // CHUNK, HEADS_PAD, CHUNK), _F32),
            pltpu.VMEM((tm // CHUNK, CHUNK, HEADS_PAD), _F32),
            pltpu.VMEM((tm // CHUNK, CHUNK, HEADS_PAD), _F32),
            pltpu.VMEM((tm, D_INNER), _BF16),
            pltpu.VMEM((tm, D_MODEL), _F32),
            pltpu.VMEM((tm, D_MODEL), _F32),
            pltpu.VMEM((D_STATE, D_INNER), _F32),
        ],
        compiler_params=pltpu.CompilerParams(
            dimension_semantics=("arbitrary", "arbitrary"),
            vmem_limit_bytes=MIXER_VMEM_LIMIT),
        name="mixer",
    )(x, *params)


def _ffn_call(h, nw, wup, fcw, fcb, wdown, fnw):
    batch, seq, _ = h.shape
    tm = FFN_TILE
    params = (nw, wup, fcw, fcb, wdown, fnw)
    in_hbm = (wup, wdown)
    assert wup.shape == (D_MODEL, 2 * D_FF) and D_MODEL % UP_STAGE_ROWS == 0
    assert wdown.shape == (D_FF, D_MODEL) and D_FF % SQUARE_STAGE_ROWS == 0
    return pl.pallas_call(
        _ffn_kernel,
        out_shape=jax.ShapeDtypeStruct(h.shape, _F32),
        grid=(batch, seq // tm),
        in_specs=[_token_tile(tm)] + [_IN_HBM if any(p is w for w in in_hbm) else _resident(p.shape)
                                      for p in params],
        out_specs=_token_tile(tm),
        scratch_shapes=[
            pltpu.VMEM(wup.shape, _BF16),
            pltpu.VMEM(wdown.shape, _BF16),
            pltpu.VMEM((STAGE_SLOTS, UP_STAGE_ROWS, 2 * D_FF), _F32),
            pltpu.VMEM((STAGE_SLOTS, SQUARE_STAGE_ROWS, D_MODEL), _F32),
            pltpu.SemaphoreType.DMA((STAGE_SLOTS,)),
            pltpu.VMEM((D_FF // LANES, HALO + tm, LANES), _F32),
            pltpu.VMEM((D_FF // LANES, tm, LANES), _F32),
        ],
        compiler_params=pltpu.CompilerParams(
            dimension_semantics=("arbitrary", "arbitrary"),
            vmem_limit_bytes=FFN_VMEM_LIMIT),
        name="ffn",
    )(h, *params)


def _row(v, pad_to=None):
    v = v.astype(_F32).reshape(1, -1)
    if pad_to is not None:
        v = jnp.pad(v, ((0, 0), (0, pad_to - v.shape[1])))
    return v


def kernel(x, norm_mix_w, w_in, conv_a_w, w_a_out, ssd_conv_w, ssd_conv_b, dt_bias, a_log, d_skip,
           ssd_norm_w, w_s_out, w_o, norm_ffn_w, w_up, ffn_conv_w, ffn_conv_b, w_down, final_norm_w):
    depth = w_in.shape[0]
    assert x.shape[1] % MIXER_TILE == 0 and x.shape[1] % FFN_TILE == 0
    assert MIXER_TILE % CHUNK == 0
    assert MIXER_TILE % (CONV_STRIDE * SUBLANES) == 0 and FFN_TILE % (CONV_STRIDE * SUBLANES) == 0
    assert HEAD_DIM & (HEAD_DIM - 1) == 0
    h = x
    for l in range(depth):
        wdt = jnp.pad(w_in[l, :, OFF_DT:N_IN], ((0, 0), (0, HEADS_PAD - N_HEADS))).astype(_BF16)
        h = _mixer_call(
            h, _row(norm_mix_w[l]), jnp.swapaxes(w_in[l], 0, 1).astype(_F32), wdt, conv_a_w[l].astype(_F32),
            w_a_out[l].astype(_F32), ssd_conv_w[l].astype(_F32), _row(ssd_conv_b[l]),
            _row(dt_bias[l], HEADS_PAD), _row(a_log[l], HEADS_PAD), _row(jnp.repeat(d_skip[l], HEAD_DIM)),
            _row(ssd_norm_w[l]), w_s_out[l].astype(_F32), w_o[l].astype(_F32))
        assert depth == 1
        h = _ffn_call(h, _row(norm_ffn_w[l]), w_up[l].astype(_F32), ffn_conv_w[l].astype(_F32),
                      _row(ffn_conv_b[l]), w_down[l].astype(_F32), _row(final_norm_w))
    return h
```

```python
import jax
import jax.numpy as jnp
from jax import lax
from jax.experimental import pallas as pl
from jax.experimental.pallas import tpu as pltpu

D_MODEL = 1024
D_CONV = D_MODEL
CONV_A_WIDTH = 3
D_INNER = 2048
HEAD_DIM = 64
N_HEADS = 32
N_GROUPS = 4
D_STATE = 128
SSD_CONV_WIDTH = 4
CHUNK = 128
D_XBC = D_INNER + 2 * N_GROUPS * D_STATE
D_FF = 2816
FFN_CONV_WIDTH = 3
EPS = 1e-5
LOG2_E = 1.4426950408889634

LANES = 128
SUBLANES = 8
CONV_STRIDE = 4

HEADS_PAD = LANES
OFF_GATE_A = 0
OFF_GATE_S = OFF_GATE_A + D_MODEL
OFF_B_A = OFF_GATE_S + D_MODEL
OFF_C_A = OFF_B_A + D_CONV
OFF_V_A = OFF_C_A + D_CONV
OFF_Z = OFF_V_A + D_CONV
OFF_XBC = OFF_Z + D_INNER
OFF_DT = OFF_XBC + D_XBC
N_IN = OFF_DT + N_HEADS

HEADS_PER_GROUP = N_HEADS // N_GROUPS
GROUP_WIDTH = HEADS_PER_GROUP * HEAD_DIM
GROUP_SLABS = GROUP_WIDTH // LANES
HALO = SUBLANES

MIXER_TILE = 256
FFN_TILE = 512
UP_STAGE_ROWS = 64
SQUARE_STAGE_ROWS = 256
STAGE_SLOTS = 4
PROJ_BLOCK = 256
MIXER_VMEM_LIMIT = 60 * 1024 * 1024
FFN_VMEM_LIMIT = 56 * 1024 * 1024

_F32 = jnp.float32
_BF16 = jnp.bfloat16


def _dot(a, b):
    return jnp.dot(a, b, preferred_element_type=_F32)


def _rmsnorm(x, w):
    return x * lax.rsqrt(jnp.mean(x * x, axis=-1, keepdims=True) + EPS) * w


def _silu(x):
    return x * jax.nn.sigmoid(x)


def _to_slabs(buf_ref, value, row0):
    rows = value.shape[0]
    for s in range(value.shape[1] // LANES):
        buf_ref[s, row0:row0 + rows, :] = value[:, s * LANES:(s + 1) * LANES]


def _from_slabs(buf_ref, rows, slabs):
    return jnp.concatenate([buf_ref[s, rows, :] for s in slabs], axis=1)


def _causal_conv_slab(buf_ref, slab, w_ref, width, rows):
    n = rows // CONV_STRIDE
    first = HALO - (width - 1)
    reads = [buf_ref[slab, pl.ds(j, n, stride=CONV_STRIDE), :]
             for j in range(first, HALO + CONV_STRIDE)]
    cols = slice(slab * LANES, (slab + 1) * LANES)
    taps = [w_ref[k:k + 1, cols] for k in range(width)]
    outs = []
    for r in range(CONV_STRIDE):
        acc = taps[0] * reads[r]
        for k in range(1, width):
            acc = acc + taps[k] * reads[r + k]
        outs.append(acc)
    return outs


def _store_strided(dst_ref, slab, outs):
    n = outs[0].shape[0]
    for r, val in enumerate(outs):
        dst_ref[slab, pl.ds(r, n, stride=CONV_STRIDE), :] = val


def _keep_halo(buf_ref, rows):
    for s in range(buf_ref.shape[0]):
        buf_ref[s, 0:HALO, :] = buf_ref[s, rows:rows + HALO, :]


def _stream_blocks(src_hbm, n_blocks, stage_ref, sem_ref, consume):
    n_slots, block_rows = stage_ref.shape[0], stage_ref.shape[1]
    lookahead = n_slots - 1

    def block_copy(i, slot):
        return pltpu.make_async_copy(src_hbm.at[pl.ds(i * block_rows, block_rows), :],
                                     stage_ref.at[slot], sem_ref.at[slot])

    for i in range(min(lookahead, n_blocks)):
        block_copy(i, i % n_slots).start()

    def body(i, carry):
        @pl.when(i + lookahead < n_blocks)
        def _():
            block_copy(i + lookahead, lax.rem(i + lookahead, n_slots)).start()

        slot = lax.rem(i, n_slots)
        block_copy(i, slot).wait()
        consume(i, stage_ref[slot])
        return carry

    lax.fori_loop(0, n_blocks, body, 0)


def _stream_cast(src_hbm, dst_ref, stage_ref, sem_ref):
    block_rows = stage_ref.shape[1]

    def consume(i, block):
        r0 = pl.multiple_of(i * block_rows, block_rows)
        dst_ref[pl.ds(r0, block_rows), :] = block.astype(_BF16)

    _stream_blocks(src_hbm, dst_ref.shape[0] // block_rows, stage_ref, sem_ref, consume)


def _stream_cast_transposed(src_hbm, dst_ref, stage_ref, sem_ref):
    def consume(i, block):
        dst_ref[i] = block.T.astype(_BF16)

    _stream_blocks(src_hbm, dst_ref.shape[0], stage_ref, sem_ref, consume)


def _first_step():
    return jnp.logical_and(pl.program_id(0) == 0, pl.program_id(1) == 0)


def _cumsum_rows(x, row_id):
    shift = 1
    while shift < x.shape[0]:
        x = x + jnp.where(row_id >= shift, pltpu.roll(x, shift, axis=0), 0.0)
        shift *= 2
    return x


def _mixer_kernel(x_ref, nw_ref, win_hbm, wdt_ref, caw_ref, waout_hbm, scw_ref, scb_ref, dtb_ref, alog_ref,
                  dskip_ref, snw_ref, wsout_hbm, wo_hbm,
                  o_ref,
                  win_ref, waout_ref, wsout_ref, wo_ref, square_stage_ref, stage_sem,
                  u_ref, cv_ref, conva_ref, yain_ref, xbc_ref, xs_ref, bc_ref, z_ref, acum_ref,
                  acumt_ref, dtend_ref, decin_ref, yn_ref, merged_ref, gs_ref, ga_ref, state_ref):
    tm = x_ref.shape[0]
    n_chunks = tm // CHUNK
    n_xs_slabs = D_INNER // LANES

    @pl.when(_first_step())
    def _():
        _stream_cast_transposed(win_hbm, win_ref, square_stage_ref, stage_sem)
        _stream_cast(waout_hbm, waout_ref, square_stage_ref, stage_sem)
        _stream_cast(wsout_hbm, wsout_ref, square_stage_ref, stage_sem)
        _stream_cast(wo_hbm, wo_ref, square_stage_ref, stage_sem)

    @pl.when(pl.program_id(1) == 0)
    def _():
        cv_ref[:, 0:HALO, :] = jnp.zeros((cv_ref.shape[0], HALO, LANES), _F32)
        xbc_ref[:, 0:HALO, :] = jnp.zeros((xbc_ref.shape[0], HALO, LANES), _F32)
        state_ref[...] = jnp.zeros(state_ref.shape, _F32)

    u_ref[...] = _rmsnorm(x_ref[...], nw_ref[...]).astype(_BF16)

    def proj(off, width):
        blocks = range(off // PROJ_BLOCK, (off + width) // PROJ_BLOCK)
        return jnp.concatenate([_dot(u_ref[...], win_ref[j]) for j in blocks], axis=1)

    a_row = -jnp.exp(alog_ref[...])
    row_id = lax.broadcasted_iota(jnp.int32, (CHUNK, CHUNK), 0)
    col_id = lax.broadcasted_iota(jnp.int32, (CHUNK, CHUNK), 1)
    causal = row_id >= col_id
    low_half = lax.broadcasted_iota(jnp.int32, (CHUNK, LANES), 1) < HEAD_DIM

    _to_slabs(xbc_ref, proj(OFF_XBC, D_XBC), HALO)

    dt_raw = _dot(u_ref[...], wdt_ref[...]) + dtb_ref[...]
    z_ref[...] = _silu(proj(OFF_Z, D_INNER))
    dt_all =jnp.maximum(dt_raw, 0.0) + jnp.log1p(jnp.exp(-jnp.abs(dt_raw)))
    for ci in range(n_chunks):
        dt = dt_all[ci * CHUNK:(ci + 1) * CHUNK, :]
        a_cum = _cumsum_rows(dt * a_row, row_id)
        a2 = a_cum * LOG2_E
        acum_ref[ci] = a2
        acumt_ref[ci] = (a2 - jnp.log(dt) * LOG2_E).T
        decay_to_end = jnp.exp2(a2[CHUNK - 1:CHUNK, :] - a2)
        decay_in = jnp.exp2(a2)
        dtend_ref[ci] = dt * decay_to_end
        decin_ref[ci] = decay_in

    for s in range(D_XBC // LANES):
        cols = slice(s * LANES, (s + 1) * LANES)
        outs = [_silu(o + scb_ref[:, cols])
                for o in _causal_conv_slab(xbc_ref, s, scw_ref, SSD_CONV_WIDTH, tm)]
        if s < n_xs_slabs:
            _store_strided(xs_ref, s, outs)
        else:
            _store_strided(bc_ref, s - n_xs_slabs, outs)
    _keep_halo(xbc_ref, tm)

    a_slabs = range(D_CONV // LANES)

    def step_c():
        _to_slabs(conva_ref, proj(OFF_C_A, D_CONV), 0)

    def step_cv():
        _to_slabs(cv_ref, proj(OFF_V_A, D_CONV) * _from_slabs(conva_ref, slice(0, tm), a_slabs), HALO)

    def step_conv_b():
        b_a = proj(OFF_B_A, D_CONV)
        for s in a_slabs:
            _store_strided(conva_ref, s, _causal_conv_slab(cv_ref, s, caw_ref, CONV_A_WIDTH, tm))
        _keep_halo(cv_ref, tm)
        yain_ref[...] = (b_a * _from_slabs(conva_ref, slice(0, tm), a_slabs)).astype(_BF16)

    def step_ya():
        merged_ref[...] = _dot(yain_ref[...], waout_ref[...])

    def step_gate_a(cols):
        ga_ref[:, cols] = jax.nn.sigmoid(proj(OFF_GATE_A + cols.start, cols.stop - cols.start))

    def step_gate_s(cols):
        gs_ref[:, cols] = jax.nn.sigmoid(proj(OFF_GATE_S + cols.start, cols.stop - cols.start))

    def step_merge(rows):
        y_s = _dot(yn_ref[rows, :], wsout_ref[...])
        merged = ga_ref[rows, :] * merged_ref[rows, :] + gs_ref[rows, :] * y_s
        yain_ref[rows, :] = merged.astype(_BF16)

    def step_out(rows):
        o_ref[rows, :] = x_ref[rows, :] + _dot(yain_ref[rows, :], wo_ref[...])

    halves = [slice(0, D_MODEL // 2), slice(D_MODEL // 2, D_MODEL)]
    gate_steps = ([lambda cols=cols: step_gate_s(cols) for cols in halves]
                  + [lambda cols=cols: step_gate_a(cols) for cols in halves])
    dense_steps = [step for pair in zip([step_c, step_cv, step_conv_b, step_ya], gate_steps) for step in pair]

    def group_prelude(ci, g):
        rows = slice(ci * CHUNK, (ci + 1) * CHUNK)
        gcols = slice(g * GROUP_WIDTH, (g + 1) * GROUP_WIDTH)
        x_end_cols, decay_in_cols = [], []
        for jp in range(GROUP_SLABS):
            h0 = g * HEADS_PER_GROUP + 2 * jp
            to_end = jnp.where(low_half, dtend_ref[ci, :, h0:h0 + 1], dtend_ref[ci, :, h0 + 1:h0 + 2])
            x_end_cols.append((xs_ref[g * GROUP_SLABS + jp, rows, :] * to_end).astype(_BF16))
            decay_in_cols.append(jnp.where(low_half, decin_ref[ci, :, h0:h0 + 1],
                                           decin_ref[ci, :, h0 + 1:h0 + 2]))
        x_end = jnp.concatenate(x_end_cols, axis=1)
        decay_in_full = jnp.concatenate(decay_in_cols, axis=1)
        b_g = bc_ref[g, rows, :].astype(_BF16)
        c_g = bc_ref[N_GROUPS + g, rows, :].astype(_BF16)
        cb = lax.dot_general(c_g, b_g, (((1,), (1,)), ((), ())), preferred_element_type=_F32)
        state_prev = state_ref[:, gcols]
        y_off = _dot(c_g, state_prev.astype(_BF16)) * decay_in_full
        if dense_steps:
            dense_steps.pop(0)()
        state_new = lax.dot_general(b_g, x_end, (((0,), (0,)), ((), ())), preferred_element_type=_F32)
        state_ref[:, gcols] = state_prev * decay_in_full[CHUNK - 1:CHUNK, :] + state_new
        return cb, y_off

    def group_body(ci, g, cb, y_off):
        rows = slice(ci * CHUNK, (ci + 1) * CHUNK)
        gcols = slice(g * GROUP_WIDTH, (g + 1) * GROUP_WIDTH)
        y_cols = []
        for jp in range(GROUP_SLABS):
            h0 = g * HEADS_PER_GROUP + 2 * jp
            pcols = slice(h0 * HEAD_DIM, (h0 + 2) * HEAD_DIM)
            lcols = slice(jp * LANES, (jp + 1) * LANES)
            decays = []
            for h in (h0, h0 + 1):
                seg = acum_ref[ci, :, h:h + 1] - acumt_ref[ci, h:h + 1, :]
                decays.append((cb * jnp.exp2(jnp.where(causal, seg, -jnp.inf))).astype(_BF16))
            lhs = jnp.concatenate(decays, axis=1)
            x_f32 = xs_ref[g * GROUP_SLABS + jp, rows, :]
            x_pair = x_f32.astype(_BF16)
            zero = jnp.zeros_like(x_pair)
            rhs = jnp.concatenate([jnp.where(low_half, x_pair, zero),
                                   jnp.where(low_half, zero, x_pair)], axis=0)
            y_cols.append(_dot(lhs, rhs) + y_off[:, lcols] + dskip_ref[:, pcols] * x_f32)
        yz = jnp.concatenate(y_cols, axis=1) * z_ref[rows, gcols]
        yn = yz * lax.rsqrt(jnp.mean(yz * yz, axis=-1, keepdims=True) + EPS) * snw_ref[:, gcols]
        yn_ref[rows, gcols] = yn.astype(_BF16)

    order = [(ci, g) for ci in range(n_chunks) for g in range(N_GROUPS)]
    ahead = group_prelude(*order[0])
    for k, (ci, g) in enumerate(order):
        current = ahead
        if k + 1 < len(order):
            ahead = group_prelude(*order[k + 1])
        group_body(ci, g, *current)
    while dense_steps:
        dense_steps.pop(0)()
    step_merge(slice(0, tm))
    step_out(slice(0, tm))


def _ffn_kernel(h_ref, nw_ref, wup_hbm, fcw_ref, fcb_ref, wdown_hbm, fnw_ref, o_ref,
                wup_ref, wdown_ref, up_stage_ref, square_stage_ref, stage_sem, h1_ref, act_ref):
    tm = h_ref.shape[0]

    @pl.when(_first_step())
    def _():
        _stream_cast(wup_hbm, wup_ref, up_stage_ref, stage_sem)
        _stream_cast(wdown_hbm, wdown_ref, square_stage_ref, stage_sem)

    @pl.when(pl.program_id(1) == 0)
    def _():
        h1_ref[:, 0:HALO, :] = jnp.zeros((h1_ref.shape[0], HALO, LANES), _F32)

    h = h_ref[...]
    v = _rmsnorm(h, nw_ref[...]).astype(_BF16)
    _to_slabs(h1_ref, _dot(v, wup_ref[:, 0:D_FF]), HALO)
    h3 = _dot(v, wup_ref[:, D_FF:2 * D_FF])
    for s in range(D_FF // LANES):
        cols = slice(s * LANES, (s + 1) * LANES)
        outs = [_silu(o + fcb_ref[:, cols])
                for o in _causal_conv_slab(h1_ref, s, fcw_ref, FFN_CONV_WIDTH, tm)]
        _store_strided(act_ref, s, outs)
    _keep_halo(h1_ref, tm)
    act = _from_slabs(act_ref, slice(0, tm), range(D_FF // LANES))
    out = h + _dot((act * h3).astype(_BF16), wdown_ref[...])
    o_ref[...] = _rmsnorm(out, fnw_ref[...])


def _resident(shape):
    return pl.BlockSpec(shape, lambda b, s: (0,) * len(shape), pipeline_mode=pl.Buffered(1))


def _token_tile(tile):
    return pl.BlockSpec((None, tile, D_MODEL), lambda b, s: (b, s, 0))


_IN_HBM = pl.BlockSpec(memory_space=pl.ANY)


def _mixer_call(x, nw, win, wdt, caw, waout, scw, scb, dtb, alog, dskip, snw, wsout, wo):
    batch, seq, _ = x.shape
    tm = MIXER_TILE
    params = (nw, win, wdt, caw, waout, scw, scb, dtb, alog, dskip, snw, wsout, wo)
    in_hbm = (win, waout, wsout, wo)
    assert win.shape == (N_IN, D_MODEL) and PROJ_BLOCK == SQUARE_STAGE_ROWS and OFF_DT % PROJ_BLOCK == 0
    assert all(w.shape[0] % SQUARE_STAGE_ROWS == 0 and w.shape[1] == D_MODEL for w in (waout, wsout, wo))
    return pl.pallas_call(
        _mixer_kernel,
        out_shape=jax.ShapeDtypeStruct(x.shape, _F32),
        grid=(batch, seq // tm),
        in_specs=[_token_tile(tm)] + [_IN_HBM if any(p is w for w in in_hbm) else _resident(p.shape)
                                      for p in params],
        out_specs=_token_tile(tm),
        scratch_shapes=[
            pltpu.VMEM((OFF_DT // PROJ_BLOCK, D_MODEL, PROJ_BLOCK), _BF16),
            pltpu.VMEM(waout.shape, _BF16),
            pltpu.VMEM(wsout.shape, _BF16),
            pltpu.VMEM(wo.shape, _BF16),
            pltpu.VMEM((STAGE_SLOTS, SQUARE_STAGE_ROWS, D_MODEL), _F32),
            pltpu.SemaphoreType.DMA((STAGE_SLOTS,)),
            pltpu.VMEM((tm, D_MODEL), _BF16),
            pltpu.VMEM((D_CONV // LANES, HALO + tm, LANES), _F32),
            pltpu.VMEM((D_CONV // LANES, tm, LANES), _F32),
            pltpu.VMEM((tm, D_CONV), _BF16),
            pltpu.VMEM((D_XBC // LANES, HALO + tm, LANES), _F32),
            pltpu.VMEM((D_INNER // LANES, tm, LANES), _F32),
            pltpu.VMEM((2 * N_GROUPS, tm, D_STATE), _F32),
            pltpu.VMEM((tm, D_INNER), _F32),
            pltpu.VMEM((tm // CHUNK, CHUNK, HEADS_PAD), _F32),
            pltpu.VMEM((tm // CHUNK, HEADS_PAD, CHUNK), _F32),
            pltpu.VMEM((tm // CHUNK, CHUNK, HEADS_PAD), _F32),
            pltpu.VMEM((tm // CHUNK, CHUNK, HEADS_PAD), _F32),
            pltpu.VMEM((tm, D_INNER), _BF16),
            pltpu.VMEM((tm, D_MODEL), _F32),
            pltpu.VMEM((tm, D_MODEL), _F32),
            pltpu.VMEM((tm, D_MODEL), _F32),
            pltpu.VMEM((D_STATE, D_INNER), _F32),
        ],
        compiler_params=pltpu.CompilerParams(
            dimension_semantics=("arbitrary", "arbitrary"),
            vmem_limit_bytes=MIXER_VMEM_LIMIT),
        name="mixer",
    )(x, *params)


def _ffn_call(h, nw, wup, fcw, fcb, wdown, fnw):
    batch, seq, _ = h.shape
    tm = FFN_TILE
    params = (nw, wup, fcw, fcb, wdown, fnw)
    in_hbm = (wup, wdown)
    assert wup.shape == (D_MODEL, 2 * D_FF) and D_MODEL % UP_STAGE_ROWS == 0
    assert wdown.shape == (D_FF, D_MODEL) and D_FF % SQUARE_STAGE_ROWS == 0
    return pl.pallas_call(
        _ffn_kernel,
        out_shape=jax.ShapeDtypeStruct(h.shape, _F32),
        grid=(batch, seq // tm),
        in_specs=[_token_tile(tm)] + [_IN_HBM if any(p is w for w in in_hbm) else _resident(p.shape)
                                      for p in params],
        out_specs=_token_tile(tm),
        scratch_shapes=[
            pltpu.VMEM(wup.shape, _BF16),
            pltpu.VMEM(wdown.shape, _BF16),
            pltpu.VMEM((STAGE_SLOTS, UP_STAGE_ROWS, 2 * D_FF), _F32),
            pltpu.VMEM((STAGE_SLOTS, SQUARE_STAGE_ROWS, D_MODEL), _F32),
            pltpu.SemaphoreType.DMA((STAGE_SLOTS,)),
            pltpu.VMEM((D_FF // LANES, HALO + tm, LANES), _F32),
            pltpu.VMEM((D_FF // LANES, tm, LANES), _F32),
        ],
        compiler_params=pltpu.CompilerParams(
            dimension_semantics=("arbitrary", "arbitrary"),
            vmem_limit_bytes=FFN_VMEM_LIMIT),
        name="ffn",
    )(h, *params)


def _row(v, pad_to=None):
    v = v.astype(_F32).reshape(1, -1)
    if pad_to is not None:
        v = jnp.pad(v, ((0, 0), (0, pad_to - v.shape[1])))
    return v


def kernel(x, norm_mix_w, w_in, conv_a_w, w_a_out, ssd_conv_w, ssd_conv_b, dt_bias, a_log, d_skip,
           ssd_norm_w, w_s_out, w_o, norm_ffn_w, w_up, ffn_conv_w, ffn_conv_b, w_down, final_norm_w):
    depth = w_in.shape[0]
    assert x.shape[1] % MIXER_TILE == 0 and x.shape[1] % FFN_TILE == 0
    assert MIXER_TILE % CHUNK == 0
    assert MIXER_TILE % (CONV_STRIDE * SUBLANES) == 0 and FFN_TILE % (CONV_STRIDE * SUBLANES) == 0
    assert HEAD_DIM & (HEAD_DIM - 1) == 0
    h = x
    for l in range(depth):
        wdt = jnp.pad(w_in[l, :, OFF_DT:N_IN], ((0, 0), (0, HEADS_PAD - N_HEADS))).astype(_BF16)
        h = _mixer_call(
            h, _row(norm_mix_w[l]), jnp.swapaxes(w_in[l], 0, 1).astype(_F32), wdt, conv_a_w[l].astype(_F32),
            w_a_out[l].astype(_F32), ssd_conv_w[l].astype(_F32), _row(ssd_conv_b[l]),
            _row(dt_bias[l], HEADS_PAD), _row(a_log[l], HEADS_PAD), _row(jnp.repeat(d_skip[l], HEAD_DIM)),
            _row(ssd_norm_w[l]), w_s_out[l].astype(_F32), w_o[l].astype(_F32))
        assert depth == 1
        h = _ffn_call(h, _row(norm_ffn_w[l]), w_up[l].astype(_F32), ffn_conv_w[l].astype(_F32),
                      _row(ffn_conv_b[l]), w_down[l].astype(_F32), _row(final_norm_w))
    return h
```
